```python
import math
import jax
import jax.numpy as jnp
from jax import lax
import numpy as np

D_MODEL = 2048
BATCH = 4
SEQ = 4096
DEPTH = 2

RMS_EPS = 1e-6
Q_BLOCK = 128

SSD_HEADS = 32
SSD_HEAD_DIM = 64
SSD_WIDTH = SSD_HEADS * SSD_HEAD_DIM
SSD_GROUPS = 4
SSD_STATE = 128
SSD_CONV = 4
SSD_CHUNK = 128
SSD_CONV_CH = SSD_WIDTH + 2 * SSD_GROUPS * SSD_STATE

MLA_HEADS = 16
MLA_Q_RANK = 512
MLA_KV_RANK = 512
MLA_NOPE = 128
MLA_ROPE = 64
MLA_V = 128
MLA_WIDTH = MLA_HEADS * MLA_V
ROPE_THETA = 10000.0

SB_HEADS = 16
SB_HEAD_DIM = 128
SB_WIDTH = SB_HEADS * SB_HEAD_DIM

FFN_HIDDEN = -(-(8 * D_MODEL) // (3 * 256)) * 256

IN_SPLITS = (SSD_WIDTH, SSD_CONV_CH, SSD_HEADS, MLA_Q_RANK, MLA_KV_RANK, MLA_ROPE)
IN_WIDTH = sum(IN_SPLITS)
IN_OFFSETS = tuple(int(v) for v in np.cumsum(IN_SPLITS)[:-1])
MIX_WIDTH = SSD_WIDTH + MLA_WIDTH

N_EVEN = (DEPTH + 1) // 2
N_ODD = DEPTH // 2

kernel_name = 'hybrid_ssd_mla_stickbreaking_block'


def rms_norm(x, g):
    xf = x.astype(jnp.float32)
    y = xf * lax.rsqrt(jnp.mean(xf * xf, axis=-1, keepdims=True) + RMS_EPS)
    return (y * g.astype(jnp.float32)).astype(x.dtype)


def rope_tables(seq, dtype):
    half = MLA_ROPE // 2
    inv_freq = ROPE_THETA ** (-jnp.arange(half, dtype=jnp.float32) / half)
    ang = jnp.arange(seq, dtype=jnp.float32)[:, None] * inv_freq[None, :]
    return jnp.cos(ang).astype(dtype), jnp.sin(ang).astype(dtype)


def apply_rope(x, cos, sin):
    x1, x2 = jnp.split(x, 2, axis=-1)
    return jnp.concatenate([x1 * cos - x2 * sin, x1 * sin + x2 * cos], axis=-1)


def swiglu_ffn(h, w_gate, w_up, w_down):
    return (jax.nn.silu(h @ w_gate) * (h @ w_up)) @ w_down


def ssd_chunked_scan(x, dt, a, b_in, c_in):
    bsz, seq, nh, hd = x.shape
    g = SSD_GROUPS
    hpg = nh // g
    l = SSD_CHUNK
    nc = seq // l
    xdt = (x * dt[..., None]).reshape(bsz, nc, l, g, hpg, hd)
    adt = (dt * a).reshape(bsz, nc, l, g, hpg).transpose(0, 3, 4, 1, 2)
    a_cs = jnp.cumsum(adt, axis=-1)
    bc = b_in.reshape(bsz, nc, l, g, SSD_STATE)
    cc = c_in.reshape(bsz, nc, l, g, SSD_STATE)
    causal = jnp.tril(jnp.ones((l, l), dtype=bool))
    seg = a_cs[..., :, None] - a_cs[..., None, :]
    decay = jnp.exp(jnp.where(causal, seg, -jnp.inf))
    cb = jnp.einsum('bclgn,bcsgn->bgcls', cc, bc)
    y_diag = jnp.einsum('bghcls,bcsghp->bclghp', cb[:, :, None] * decay, xdt)
    decay_to_end = jnp.exp(a_cs[..., -1:] - a_cs)
    states = jnp.einsum('bclgn,bghcl,bclghp->bcghpn', bc, decay_to_end, xdt)
    chunk_decay = jnp.exp(a_cs[..., -1])

    def step(carry, inp):
        s_c, d_c = inp
        return carry * d_c[..., None, None] + s_c, carry

    init = jnp.zeros(states.shape[:1] + states.shape[2:], states.dtype)
    _, prev = lax.scan(step, init, (jnp.moveaxis(states, 1, 0), jnp.moveaxis(chunk_decay, 3, 0)))
    prev = jnp.moveaxis(prev, 0, 1)
    y_off = jnp.einsum('bclgn,bcghpn,bghcl->bclghp', cc, prev, jnp.exp(a_cs))
    return (y_diag + y_off).reshape(bsz, seq, nh, hd)


def ssd_mixer(z, xbc, dt_raw, conv_w, conv_b, dt_bias, a_log, d_skip, norm_g):
    bsz, seq, _ = xbc.shape
    xbc = lax.conv_general_dilated(
        xbc, conv_w[:, None, :].astype(xbc.dtype), window_strides=(1,),
        padding=[(SSD_CONV - 1, 0)], dimension_numbers=('NWC', 'WIO', 'NWC'),
        feature_group_count=SSD_CONV_CH)
    xbc = jax.nn.silu(xbc + conv_b).astype(jnp.float32)
    x_ssm, b_in, c_in = jnp.split(xbc, [SSD_WIDTH, SSD_WIDTH + SSD_GROUPS * SSD_STATE], axis=-1)
    x_ssm = x_ssm.reshape(bsz, seq, SSD_HEADS, SSD_HEAD_DIM)
    b_in = b_in.reshape(bsz, seq, SSD_GROUPS, SSD_STATE)
    c_in = c_in.reshape(bsz, seq, SSD_GROUPS, SSD_STATE)
    dt = jax.nn.softplus(dt_raw.astype(jnp.float32) + dt_bias.astype(jnp.float32))
    a = -jnp.exp(a_log.astype(jnp.float32))
    y = ssd_chunked_scan(x_ssm, dt, a, b_in, c_in) + d_skip.astype(jnp.float32)[:, None] * x_ssm
    y = y.reshape(bsz, seq, SSD_WIDTH)
    return rms_norm(y * jax.nn.silu(z.astype(jnp.float32)), norm_g).astype(z.dtype)


def mla_mixer(c_q, c_kv, k_rope, q_norm, kv_norm, w_uq, w_ukv, cos, sin):
    bsz, seq, _ = c_q.shape
    q = (rms_norm(c_q, q_norm) @ w_uq).reshape(bsz, seq, MLA_HEADS, MLA_NOPE + MLA_ROPE)
    kv = (rms_norm(c_kv, kv_norm) @ w_ukv).reshape(bsz, seq, MLA_HEADS, MLA_NOPE + MLA_V)
    q_nope, q_pe = jnp.split(q, [MLA_NOPE], axis=-1)
    k_nope, v = jnp.split(kv, [MLA_NOPE], axis=-1)
    q_pe = apply_rope(q_pe, cos[:, None, :], sin[:, None, :])
    k_pe = apply_rope(k_rope, cos, sin)
    scale = (MLA_NOPE + MLA_ROPE) ** -0.5
    outs = []
    for i in range(seq // Q_BLOCK):
        q0, q1 = i * Q_BLOCK, (i + 1) * Q_BLOCK
        s = (jnp.einsum('bqhd,bkhd->bhqk', q_nope[:, q0:q1], k_nope[:, :q1])
             + jnp.einsum('bqhr,bkr->bhqk', q_pe[:, q0:q1], k_pe[:, :q1])).astype(jnp.float32) * scale
        mask = jnp.arange(q1)[None, :] <= jnp.arange(q0, q1)[:, None]
        p = jax.nn.softmax(jnp.where(mask, s, -jnp.inf), axis=-1)
        outs.append(jnp.einsum('bhqk,bkhd->bqhd', p.astype(v.dtype), v[:, :q1]))
    return jnp.concatenate(outs, axis=1).reshape(bsz, seq, MLA_WIDTH)


def ssd_mla_mixer(h, w_in, conv_w, conv_b, dt_bias, a_log, d_skip, ssd_norm,
                  q_norm, kv_norm, w_uq, w_ukv, w_out, cos, sin):
    z, xbc, dt_raw, c_q, c_kv, k_rope = jnp.split(h @ w_in, IN_OFFSETS, axis=-1)
    y_ssd = ssd_mixer(z, xbc, dt_raw, conv_w, conv_b, dt_bias, a_log, d_skip, ssd_norm)
    y_mla = mla_mixer(c_q, c_kv, k_rope, q_norm, kv_norm, w_uq, w_ukv, cos, sin)
    return jnp.concatenate([y_ssd, y_mla], axis=-1) @ w_out


def stick_breaking_mixer(h, w_qkv, w_out):
    bsz, seq, _ = h.shape
    qkv = (h @ w_qkv).reshape(bsz, seq, 3, SB_HEADS, SB_HEAD_DIM)
    q, k, v = qkv[:, :, 0], qkv[:, :, 1], qkv[:, :, 2]
    scale = SB_HEAD_DIM ** -0.5
    outs = []
    for i in range(seq // Q_BLOCK):
        q0, q1 = i * Q_BLOCK, (i + 1) * Q_BLOCK
        z = jnp.einsum('bqhd,bkhd->bhqk', q[:, q0:q1], k[:, :q1]).astype(jnp.float32) * scale
        mask = jnp.arange(q1)[None, :] < jnp.arange(q0, q1)[:, None]
        log_keep = jnp.where(mask, -jax.nn.softplus(z), 0.0)
        later = lax.cumsum(log_keep, axis=log_keep.ndim - 1, reverse=True) - log_keep
        weight = jnp.where(mask, jnp.exp(jax.nn.log_sigmoid(z) + later), 0.0)
        outs.append(jnp.einsum('bhqk,bkhd->bqhd', weight.astype(v.dtype), v[:, :q1]))
    return jnp.concatenate(outs, axis=1).reshape(bsz, seq, SB_WIDTH) @ w_out


def setup_inputs(seed: int = 0) -> dict:
    key = jax.random.key(seed)
    ks = jax.random.split(key, 24)
    f32 = jnp.float32

    def dense(k, shape, fan_in):
        return jax.random.normal(k, shape, f32) * fan_in ** -0.5

    def gain(k, shape):
        return 1.0 + 0.02 * jax.random.normal(k, shape, f32)

    dt0 = jnp.exp(jax.random.uniform(ks[6], (N_EVEN, SSD_HEADS), f32, math.log(1e-3), math.log(1e-1)))
    return {
        'x': jax.random.normal(ks[0], (BATCH, SEQ, D_MODEL), f32),
        'mix_norm': gain(ks[1], (DEPTH, D_MODEL)),
        'ffn_norm': gain(ks[2], (DEPTH, D_MODEL)),
        'w_in': dense(ks[3], (N_EVEN, D_MODEL, IN_WIDTH), D_MODEL),
        'conv_w': dense(ks[4], (N_EVEN, SSD_CONV, SSD_CONV_CH), SSD_CONV),
        'conv_b': 0.02 * jax.random.normal(ks[5], (N_EVEN, SSD_CONV_CH), f32),
        'dt_bias': dt0 + jnp.log(-jnp.expm1(-dt0)),
        'a_log': jnp.log(jax.random.uniform(ks[7], (N_EVEN, SSD_HEADS), f32, 1.0, 16.0)),
        'd_skip': gain(ks[8], (N_EVEN, SSD_HEADS)),
        'ssd_norm': gain(ks[9], (N_EVEN, SSD_WIDTH)),
        'q_norm': gain(ks[10], (N_EVEN, MLA_Q_RANK)),
        'kv_norm': gain(ks[11], (N_EVEN, MLA_KV_RANK)),
        'w_uq': dense(ks[12], (N_EVEN, MLA_Q_RANK, MLA_HEADS * (MLA_NOPE + MLA_ROPE)), MLA_Q_RANK),
        'w_ukv': dense(ks[13], (N_EVEN, MLA_KV_RANK, MLA_HEADS * (MLA_NOPE + MLA_V)), MLA_KV_RANK),
        'w_out_even': dense(ks[14], (N_EVEN, MIX_WIDTH, D_MODEL), MIX_WIDTH),
        'w_qkv': dense(ks[15], (N_ODD, D_MODEL, 3 * SB_WIDTH), D_MODEL),
        'w_out_odd': dense(ks[16], (N_ODD, SB_WIDTH, D_MODEL), SB_WIDTH),
        'w_gate': dense(ks[17], (DEPTH, D_MODEL, FFN_HIDDEN), D_MODEL),
        'w_up': dense(ks[18], (DEPTH, D_MODEL, FFN_HIDDEN), D_MODEL),
        'w_down': dense(ks[19], (DEPTH, FFN_HIDDEN, D_MODEL), FFN_HIDDEN),
        'final_norm': gain(ks[20], (D_MODEL,)),
    }


def reference(x, mix_norm, ffn_norm, w_in, conv_w, conv_b, dt_bias, a_log, d_skip, ssd_norm,
              q_norm, kv_norm, w_uq, w_ukv, w_out_even, w_qkv, w_out_odd,
              w_gate, w_up, w_down, final_norm):
    cos, sin = rope_tables(x.shape[1], x.dtype)
    for layer in range(DEPTH):
        h = rms_norm(x, mix_norm[layer])
        if layer % 2 == 0:
            e = layer // 2
            mix = ssd_mla_mixer(h, w_in[e], conv_w[e], conv_b[e], dt_bias[e], a_log[e], d_skip[e],
                                ssd_norm[e], q_norm[e], kv_norm[e], w_uq[e], w_ukv[e],
                                w_out_even[e], cos, sin)
        else:
            o = layer // 2
            mix = stick_breaking_mixer(h, w_qkv[o], w_out_odd[o])
        x = x + mix
        x = x + swiglu_ffn(rms_norm(x, ffn_norm[layer]), w_gate[layer], w_up[layer], w_down[layer])
    return rms_norm(x, final_norm)
```

```python
import functools

import jax
import jax.numpy as jnp
import numpy as np
from jax import lax
from jax.experimental import pallas as pl
from jax.experimental.pallas import tpu as pltpu

F32 = jnp.float32
BF16 = jnp.bfloat16

RMS_EPS = 1e-6
ROPE_THETA = 10000.0

SSD_HEADS = 32
SSD_HEAD_DIM = 64
SSD_WIDTH = SSD_HEADS * SSD_HEAD_DIM
SSD_GROUPS = 4
SSD_STATE = 128
SSD_CONV = 4
SSD_CHUNK = 128
SSD_CONV_CH = SSD_WIDTH + 2 * SSD_GROUPS * SSD_STATE

MLA_HEADS = 16
MLA_RANK = 512
MLA_NOPE = 128
MLA_ROPE = 64
MLA_V = 128

SB_HEADS = 16
SB_HEAD_DIM = 128

LANES = 128
CONV_TAIL = 8
VMEM_LIMIT = 56 * 1024 * 1024
SB_DEAD = 104.0
NEG_BIG = -1e30

NT_DIMS = (((1,), (1,)), ((), ()))


def _cparams(*sem):
    return pltpu.CompilerParams(dimension_semantics=sem, vmem_limit_bytes=VMEM_LIMIT)


def _dot(a, b):
    return jnp.dot(a, b, preferred_element_type=F32)


def _dot_nt(a, b):
    return lax.dot_general(a, b, NT_DIMS, preferred_element_type=F32)


def _rms(x, g):
    ms = jnp.mean(x * x, axis=-1, keepdims=True)
    return x * lax.rsqrt(ms + RMS_EPS) * g


def _silu(x):
    return x * jax.nn.sigmoid(x)


def _softplus(x):
    return jnp.maximum(x, 0.0) + jnp.log1p(jnp.exp(-jnp.abs(x)))


def _split3_dot(tri, x):
    x1 = x.astype(BF16)
    r1 = x - x1.astype(F32)
    x2 = r1.astype(BF16)
    x3 = (r1 - x2.astype(F32)).astype(BF16)
    return _dot(tri, x1) + _dot(tri, x2) + _dot(tri, x3)


def _norm_rows_into(x_ref, g_ref, h_scr, rows):
    tm = x_ref.shape[0]
    for r in range(0, tm, rows):
        x = x_ref[r:r + rows, :].astype(F32)
        h_scr[r:r + rows, :] = _rms(x, g_ref[...]).astype(BF16)


def _norm_mm_kernel(x_ref, g_ref, w_ref, o_ref, h_scr, *, rows):
    @pl.when(pl.program_id(1) == 0)
    def _():
        _norm_rows_into(x_ref, g_ref, h_scr, rows)

    o_ref[...] = _dot(h_scr[...], w_ref[...]).astype(o_ref.dtype)


def _norm_mm2_kernel(x_ref, g_ref, w_ref, ws_ref, o_ref, os_ref, h_scr, *, rows):
    @pl.when(pl.program_id(1) == 0)
    def _():
        _norm_rows_into(x_ref, g_ref, h_scr, rows)
        os_ref[...] = _dot(h_scr[...], ws_ref[...])

    o_ref[...] = _dot(h_scr[...], w_ref[...]).astype(o_ref.dtype)


def norm_matmul(x, g, w, w_small=None, *, tm=1024, tn=512):
    t, k = x.shape
    n = w.shape[1]
    tm = min(tm, t)
    rows = min(256, tm)
    grid = (t // tm, n // tn)
    x_spec = pl.BlockSpec((tm, k), lambda i, j: (i, 0))
    g_spec = pl.BlockSpec((1, k), lambda i, j: (0, 0))
    w_spec = pl.BlockSpec((k, tn), lambda i, j: (0, j))
    o_spec = pl.BlockSpec((tm, tn), lambda i, j: (i, j))
    scratch = [pltpu.VMEM((tm, k), BF16)]
    if w_small is None:
        return pl.pallas_call(
            functools.partial(_norm_mm_kernel, rows=rows),
            grid=grid, in_specs=[x_spec, g_spec, w_spec], out_specs=o_spec,
            out_shape=jax.ShapeDtypeStruct((t, n), BF16), scratch_shapes=scratch,
            compiler_params=_cparams("parallel", "arbitrary"),
        )(x, g, w)
    ns = w_small.shape[1]
    return pl.pallas_call(
        functools.partial(_norm_mm2_kernel, rows=rows),
        grid=grid,
        in_specs=[x_spec, g_spec, w_spec, pl.BlockSpec((k, ns), lambda i, j: (0, 0))],
        out_specs=[o_spec, pl.BlockSpec((tm, ns), lambda i, j: (i, 0))],
        out_shape=[jax.ShapeDtypeStruct((t, n), BF16), jax.ShapeDtypeStruct((t, ns), F32)],
        scratch_shapes=scratch,
        compiler_params=_cparams("parallel", "arbitrary"),
    )(x, g, w, w_small)


def _ffn_up_kernel(x_ref, g_ref, wg_ref, wu_ref, o_ref, h_scr, *, rows):
    @pl.when(pl.program_id(1) == 0)
    def _():
        _norm_rows_into(x_ref, g_ref, h_scr, rows)

    h = h_scr[...]
    a = _dot(h, wg_ref[...])
    b = _dot(h, wu_ref[...])
    o_ref[...] = (_silu(a) * b).astype(o_ref.dtype)


def ffn_up(x, g, wg, wu, *, tm=1024, tn=512):
    t, k = x.shape
    n = wg.shape[1]
    tm = min(tm, t)
    rows = min(256, tm)
    return pl.pallas_call(
        functools.partial(_ffn_up_kernel, rows=rows),
        grid=(t // tm, n // tn),
        in_specs=[pl.BlockSpec((tm, k), lambda i, j: (i, 0)),
                  pl.BlockSpec((1, k), lambda i, j: (0, 0)),
                  pl.BlockSpec((k, tn), lambda i, j: (0, j)),
                  pl.BlockSpec((k, tn), lambda i, j: (0, j))],
        out_specs=pl.BlockSpec((tm, tn), lambda i, j: (i, j)),
        out_shape=jax.ShapeDtypeStruct((t, n), BF16),
        scratch_shapes=[pltpu.VMEM((tm, k), BF16)],
        compiler_params=_cparams("parallel", "arbitrary"),
    )(x, g, wg, wu)


def _mm_res_kernel(*refs, n_pairs):
    a_refs = refs[:n_pairs]
    w_refs = refs[n_pairs:2 * n_pairs]
    r_ref, o_ref = refs[2 * n_pairs], refs[2 * n_pairs + 1]
    acc = r_ref[...]
    for a_ref, w_ref in zip(a_refs, w_refs):
        acc = acc + _dot(a_ref[...], w_ref[...])
    o_ref[...] = acc


def matmul_residual(a_list, w_list, res, *, tm=512, tn=512):
    t, n = res.shape
    tm = min(tm, t)
    n_pairs = len(a_list)
    in_specs = [pl.BlockSpec((tm, a.shape[1]), lambda i, j: (i, 0)) for a in a_list]
    in_specs += [pl.BlockSpec((w.shape[0], tn), lambda i, j: (0, j)) for w in w_list]
    in_specs += [pl.BlockSpec((tm, tn), lambda i, j: (i, j))]
    return pl.pallas_call(
        functools.partial(_mm_res_kernel, n_pairs=n_pairs),
        grid=(t // tm, n // tn),
        in_specs=in_specs,
        out_specs=pl.BlockSpec((tm, tn), lambda i, j: (i, j)),
        out_shape=jax.ShapeDtypeStruct((t, n), F32),
        compiler_params=_cparams("parallel", "parallel"),
    )(*a_list, *w_list, res)


def _final_norm_kernel(x_ref, g_ref, o_ref):
    o_ref[...] = _rms(x_ref[...], g_ref[...])


def final_rmsnorm(x, g, *, tm=256):
    t, k = x.shape
    tm = min(tm, t)
    return pl.pallas_call(
        _final_norm_kernel,
        grid=(t // tm,),
        in_specs=[pl.BlockSpec((tm, k), lambda i: (i, 0)), pl.BlockSpec((1, k), lambda i: (0, 0))],
        out_specs=pl.BlockSpec((tm, k), lambda i: (i, 0)),
        out_shape=jax.ShapeDtypeStruct((t, k), F32),
        compiler_params=_cparams("parallel"),
    )(x, g)


def _pair_cols(arr, h0, lane_lo):
    rows = arr.shape[0]
    c0 = jnp.broadcast_to(arr[:, h0:h0 + 1], (rows, LANES))
    c1 = jnp.broadcast_to(arr[:, h0 + 1:h0 + 2], (rows, LANES))
    return jnp.where(lane_lo, c0, c1)


def _ssd_kernel(z_ref, xbc_ref, dt_ref, cw_ref, cb_ref, dtb_ref, alog_ref, dsk_ref, g_ref, tril_ref,
                o_ref, xext, xc, state, y_scr):
    L = SSD_CHUNK
    hp = SSD_HEAD_DIM
    gw = (SSD_HEADS // SSD_GROUPS) * hp
    b_off = SSD_WIDTH
    c_off = SSD_WIDTH + SSD_GROUPS * SSD_STATE

    @pl.when(pl.program_id(1) == 0)
    def _():
        xext[0:CONV_TAIL, :] = jnp.zeros((CONV_TAIL, SSD_CONV_CH), F32)
        state[...] = jnp.zeros_like(state)

    xext[CONV_TAIL:CONV_TAIL + L, :] = xbc_ref[...].astype(F32)
    cblk = 512
    for c0 in range(0, SSD_CONV_CH, cblk):
        cs = slice(c0, c0 + cblk)
        acc = cb_ref[:, cs] + cw_ref[0:1, cs] * xext[CONV_TAIL - 3:CONV_TAIL - 3 + L, cs]
        for k in range(1, SSD_CONV):
            r0 = CONV_TAIL - 3 + k
            acc = acc + cw_ref[k:k + 1, cs] * xext[r0:r0 + L, cs]
        xc[:, cs] = _silu(acc)
    xext[0:CONV_TAIL, :] = xext[L:L + CONV_TAIL, :]

    dt = _softplus(dt_ref[...] + dtb_ref[...])
    a = -jnp.exp(alog_ref[...])
    a_cs = _split3_dot(tril_ref[...], dt * a)
    a_cs_t = a_cs.T
    a_end = a_cs[L - 1:L, :]
    exp_a = jnp.exp(a_cs)
    dt_to_end = dt * jnp.exp(a_end - a_cs)
    chunk_decay = jnp.exp(a_end)

    row = lax.broadcasted_iota(jnp.int32, (L, L), 0)
    col = lax.broadcasted_iota(jnp.int32, (L, L), 1)
    causal = col <= row
    lane_lo = lax.broadcasted_iota(jnp.int32, (L, LANES), 1) < hp
    lane_lo_row = lax.broadcasted_iota(jnp.int32, (1, LANES), 1) < hp

    for g in range(SSD_GROUPS):
        b_g = xc[:, b_off + g * SSD_STATE:b_off + (g + 1) * SSD_STATE]
        c_g = xc[:, c_off + g * SSD_STATE:c_off + (g + 1) * SSD_STATE].astype(BF16)
        cb = _dot_nt(c_g, b_g.astype(BF16))
        b_g_t = b_g.T.astype(BF16)
        y_off = _dot(c_g, state[:, g * gw:(g + 1) * gw].astype(BF16))
        for k in range(gw // LANES):
            h0 = g * (SSD_HEADS // SSD_GROUPS) + 2 * k
            cs = slice(g * gw + k * LANES, g * gw + (k + 1) * LANES)
            x_pair = xc[:, cs]
            xdt = (x_pair * _pair_cols(dt, h0, lane_lo)).astype(BF16)
            ys = []
            for h in (h0, h0 + 1):
                seg = jnp.broadcast_to(a_cs[:, h:h + 1], (L, L)) - jnp.broadcast_to(a_cs_t[h:h + 1, :], (L, L))
                decay = jnp.exp(jnp.where(causal, seg, NEG_BIG))
                ys.append(_dot((cb * decay).astype(BF16), xdt))
            y = jnp.where(lane_lo, ys[0], ys[1])
            y = y + y_off[:, k * LANES:(k + 1) * LANES] * _pair_cols(exp_a, h0, lane_lo)
            y_scr[:, cs] = y + dsk_ref[:, cs] * x_pair
            x_to_end = (x_pair * _pair_cols(dt_to_end, h0, lane_lo)).astype(BF16)
            cd = jnp.where(lane_lo_row,
                           jnp.broadcast_to(chunk_decay[:, h0:h0 + 1], (1, LANES)),
                           jnp.broadcast_to(chunk_decay[:, h0 + 1:h0 + 2], (1, LANES)))
            state[:, cs] = state[:, cs] * cd + _dot(b_g_t, x_to_end)

    zz = z_ref[...].astype(F32)
    o_ref[...] = _rms(y_scr[...] * _silu(zz), g_ref[...]).astype(o_ref.dtype)


def ssd_mixer(proj, small, conv_w, conv_b, dt_bias_pad, a_log_pad, d_skip_rep, norm_g, *, batch, seq,
              z_blk, xbc_blk, dt_blk):
    L = SSD_CHUNK
    nc = seq // L
    tril = jnp.asarray(np.tril(np.ones((L, L), np.float32)), BF16)
    const = lambda b, c: (0, 0)
    return pl.pallas_call(
        _ssd_kernel,
        grid=(batch, nc),
        in_specs=[pl.BlockSpec((L, SSD_WIDTH), lambda b, c: (b * nc + c, z_blk)),
                  pl.BlockSpec((L, SSD_CONV_CH), lambda b, c: (b * nc + c, xbc_blk)),
                  pl.BlockSpec((L, LANES), lambda b, c: (b * nc + c, dt_blk)),
                  pl.BlockSpec((SSD_CONV, SSD_CONV_CH), const),
                  pl.BlockSpec((1, SSD_CONV_CH), const),
                  pl.BlockSpec((1, LANES), const),
                  pl.BlockSpec((1, LANES), const),
                  pl.BlockSpec((1, SSD_WIDTH), const),
                  pl.BlockSpec((1, SSD_WIDTH), const),
                  pl.BlockSpec((L, L), const)],
        out_specs=pl.BlockSpec((L, SSD_WIDTH), lambda b, c: (b * nc + c, 0)),
        out_shape=jax.ShapeDtypeStruct((batch * seq, SSD_WIDTH), BF16),
        scratch_shapes=[pltpu.VMEM((CONV_TAIL + L, SSD_CONV_CH), F32),
                        pltpu.VMEM((L, SSD_CONV_CH), F32),
                        pltpu.VMEM((SSD_STATE, SSD_WIDTH), F32),
                        pltpu.VMEM((L, SSD_WIDTH), F32)],
        compiler_params=_cparams("parallel", "arbitrary"),
    )(proj, proj, small, conv_w, conv_b, dt_bias_pad, a_log_pad, d_skip_rep, norm_g, tril)


def _rope(x, c, sa, sb):
    half = MLA_ROPE // 2
    return x * c + pltpu.roll(x, LANES - half, 1) * sa + pltpu.roll(x, half, 1) * sb


def _mla_prep_kernel(cq_ref, ckv_ref, kr_ref, qg_ref, kvg_ref, wq_ref, wkv_ref, c_ref, sa_ref, sb_ref,
                     qn_ref, qp_ref, kv_ref, kp_ref):
    scale = float((MLA_NOPE + MLA_ROPE) ** -0.5)
    hq = _rms(cq_ref[...].astype(F32), qg_ref[...]).astype(BF16)
    hkv = _rms(ckv_ref[...].astype(F32), kvg_ref[...]).astype(BF16)
    c, sa, sb = c_ref[...], sa_ref[...], sb_ref[...]
    nope_w = MLA_HEADS * MLA_NOPE
    nblk = 512
    for n in range(0, nope_w, nblk):
        qn_ref[:, n:n + nblk] = (_dot(hq, wq_ref[:, n:n + nblk]) * scale).astype(BF16)
    for n in range(0, MLA_HEADS * LANES, nblk):
        qp = _dot(hq, wq_ref[:, nope_w + n:nope_w + n + nblk])
        for s in range(0, nblk, LANES):
            qp_ref[:, n + s:n + s + LANES] = (_rope(qp[:, s:s + LANES], c, sa, sb) * scale).astype(BF16)
    for n in range(0, MLA_HEADS * (MLA_NOPE + MLA_V), nblk):
        kv_ref[:, n:n + nblk] = _dot(hkv, wkv_ref[:, n:n + nblk]).astype(BF16)
    kp_ref[...] = _rope(kr_ref[...], c, sa, sb).astype(BF16)


def mla_prep(proj, small, q_norm, kv_norm, wq, wkv, rope_c, rope_sa, rope_sb, *, seq, cq_blk, ckv_blk,
             kr_blk, tm=512):
    t = proj.shape[0]
    tm = min(tm, seq)
    npos = seq // tm
    const = lambda i: (0, 0)
    pos = lambda i: (i % npos, 0)
    qw = MLA_HEADS * MLA_NOPE
    kvw = MLA_HEADS * (MLA_NOPE + MLA_V)
    return pl.pallas_call(
        _mla_prep_kernel,
        grid=(t // tm,),
        in_specs=[pl.BlockSpec((tm, MLA_RANK), lambda i: (i, cq_blk)),
                  pl.BlockSpec((tm, MLA_RANK), lambda i: (i, ckv_blk)),
                  pl.BlockSpec((tm, LANES), lambda i: (i, kr_blk)),
                  pl.BlockSpec((1, MLA_RANK), const),
                  pl.BlockSpec((1, MLA_RANK), const),
                  pl.BlockSpec(wq.shape, const),
                  pl.BlockSpec(wkv.shape, const),
                  pl.BlockSpec((tm, LANES), pos),
                  pl.BlockSpec((tm, LANES), pos),
                  pl.BlockSpec((tm, LANES), pos)],
        out_specs=[pl.BlockSpec((tm, qw), lambda i: (i, 0)),
                   pl.BlockSpec((tm, MLA_HEADS * LANES), lambda i: (i, 0)),
                   pl.BlockSpec((tm, kvw), lambda i: (i, 0)),
                   pl.BlockSpec((tm, LANES), lambda i: (i, 0))],
        out_shape=[jax.ShapeDtypeStruct((t, qw), BF16),
                   jax.ShapeDtypeStruct((t, MLA_HEADS * LANES), BF16),
                   jax.ShapeDtypeStruct((t, kvw), BF16),
                   jax.ShapeDtypeStruct((t, LANES), BF16)],
        compiler_params=_cparams("parallel"),
    )(proj, proj, small, q_norm, kv_norm, wq, wkv, rope_c, rope_sa, rope_sb)


def _mla_attn_kernel(qn_ref, qp_ref, kn_ref, kp_ref, v_ref, o_ref, *, tq):
    i = pl.program_id(2)
    q = jnp.concatenate([qn_ref[...], qp_ref[...]], axis=1)

    def block(kb, carry, masked):
        m, l, acc = carry
        ks = pl.multiple_of(kb * tq, tq)
        k = jnp.concatenate([kn_ref[pl.ds(ks, tq), :], kp_ref[pl.ds(ks, tq), :]], axis=1)
        s = _dot_nt(q, k)
        if masked:
            row = lax.broadcasted_iota(jnp.int32, (tq, tq), 0)
            col = lax.broadcasted_iota(jnp.int32, (tq, tq), 1)
            s = jnp.where(col <= row, s, NEG_BIG)
        m_new = jnp.maximum(m, jnp.max(s, axis=1, keepdims=True))
        alpha = jnp.exp(m - m_new)
        p = jnp.exp(s - m_new)
        l = alpha * l + jnp.sum(p, axis=1, keepdims=True)
        acc = alpha * acc + _dot(p.astype(BF16), v_ref[pl.ds(ks, tq), :])
        return m_new, l, acc

    init = (jnp.full((tq, 1), NEG_BIG, F32), jnp.zeros((tq, 1), F32), jnp.zeros((tq, MLA_V), F32))
    carry = lax.fori_loop(0, i, lambda kb, c: block(kb, c, False), init)
    _, l, acc = block(i, carry, True)
    o_ref[...] = (acc / l).astype(o_ref.dtype)


def mla_attention(qn, qp, kv, kp, *, batch, seq, tq=512):
    tq = min(tq, seq)
    nq = seq // tq
    t = batch * seq
    return pl.pallas_call(
        functools.partial(_mla_attn_kernel, tq=tq),
        grid=(batch, MLA_HEADS, nq),
        in_specs=[pl.BlockSpec((tq, LANES), lambda b, h, i: (b * nq + i, h)),
                  pl.BlockSpec((tq, LANES), lambda b, h, i: (b * nq + i, h)),
                  pl.BlockSpec((seq, LANES), lambda b, h, i: (b, 2 * h)),
                  pl.BlockSpec((seq, LANES), lambda b, h, i: (b, 0)),
                  pl.BlockSpec((seq, LANES), lambda b, h, i: (b, 2 * h + 1))],
        out_specs=pl.BlockSpec((tq, LANES), lambda b, h, i: (b * nq + i, h)),
        out_shape=jax.ShapeDtypeStruct((t, MLA_HEADS * MLA_V), BF16),
        compiler_params=_cparams("parallel", "parallel", "arbitrary"),
    )(qn, qp, kv, kp, kv)


def _sb_attn_kernel(q_ref, k_ref, v_ref, u_ref, o_ref, *, tq):
    i = pl.program_id(2)
    q = q_ref[...]
    u = u_ref[...]

    def block(kb, carry, acc, masked):
        ks = pl.multiple_of(kb * tq, tq)
        z = _dot_nt(q, k_ref[pl.ds(ks, tq), :])
        sp = _softplus(z)
        if masked:
            row = lax.broadcasted_iota(jnp.int32, (tq, tq), 0)
            col = lax.broadcasted_iota(jnp.int32, (tq, tq), 1)
            keep = col < row
            sp = jnp.where(keep, sp, 0.0)
        hi = sp.astype(BF16)
        lo = (sp - hi.astype(F32)).astype(BF16)
        rsum = _dot(hi, u) + _dot(lo, u)
        w = jnp.exp(z - rsum - carry)
        if masked:
            w = jnp.where(keep, w, 0.0)
        acc = acc + _dot(w.astype(BF16), v_ref[pl.ds(ks, tq), :])
        carry = carry + rsum[:, 0:1]
        return carry, acc

    carry, acc = block(i, jnp.zeros((tq, 1), F32), jnp.zeros((tq, SB_HEAD_DIM), F32), True)

    def cond(st):
        kb, live, _, _ = st
        return jnp.logical_and(kb >= 0, live < SB_DEAD)

    def body(st):
        kb, _, carry, acc = st
        carry, acc = block(kb, carry, acc, False)
        return kb - 1, jnp.min(carry), carry, acc

    _, _, _, acc = lax.while_loop(cond, body, (i - 1, jnp.min(carry), carry, acc))
    o_ref[...] = acc.astype(o_ref.dtype)


def sb_attention(qkv, *, batch, seq, tq=256):
    tq = min(tq, seq)
    nq = seq // tq
    t = batch * seq
    u = jnp.asarray(np.tril(np.ones((tq, tq), np.float32)), BF16)
    return pl.pallas_call(
        functools.partial(_sb_attn_kernel, tq=tq),
        grid=(batch, SB_HEADS, nq),
        in_specs=[pl.BlockSpec((tq, LANES), lambda b, h, i: (b * nq + i, h)),
                  pl.BlockSpec((seq, LANES), lambda b, h, i: (b, SB_HEADS + h)),
                  pl.BlockSpec((seq, LANES), lambda b, h, i: (b, 2 * SB_HEADS + h)),
                  pl.BlockSpec((tq, tq), lambda b, h, i: (0, 0))],
        out_specs=pl.BlockSpec((tq, LANES), lambda b, h, i: (b * nq + i, h)),
        out_shape=jax.ShapeDtypeStruct((t, SB_HEADS * SB_HEAD_DIM), BF16),
        compiler_params=_cparams("parallel", "parallel", "arbitrary"),
    )(qkv, qkv, qkv, u)


def _rope_tables(seq):
    half = MLA_ROPE // 2
    inv_freq = ROPE_THETA ** (-jnp.arange(half, dtype=F32) / half)
    ang = jnp.arange(seq, dtype=F32)[:, None] * inv_freq[None, :]
    cos, sin = jnp.cos(ang), jnp.sin(ang)
    zero = jnp.zeros_like(cos)
    pad = jnp.zeros((seq, LANES - MLA_ROPE), F32)
    c = jnp.concatenate([cos, cos, pad], axis=1)
    sa = jnp.concatenate([-sin, zero, pad], axis=1)
    sb = jnp.concatenate([zero, sin, pad], axis=1)
    return c, sa, sb


def _pad_lanes(v, width=LANES):
    v = v.reshape(1, -1).astype(F32)
    return jnp.pad(v, ((0, 0), (0, width - v.shape[1])))


def _even_layer(x, mix_g, w_in, conv_w, conv_b, dt_bias, a_log, d_skip, ssd_norm, q_norm, kv_norm,
                w_uq, w_ukv, w_out, rope, *, batch, seq):
    d = x.shape[1]
    o_z, o_xbc, o_dt, o_cq, o_ckv, o_kr = np.cumsum(
        [0, SSD_WIDTH, SSD_CONV_CH, SSD_HEADS, MLA_RANK, MLA_RANK]).tolist()
    w_main = jnp.concatenate([w_in[:, o_z:o_xbc], w_in[:, o_cq:o_ckv], w_in[:, o_ckv:o_kr],
                              w_in[:, o_xbc:o_dt]], axis=1).astype(BF16)
    zeros = lambda n: jnp.zeros((d, n), F32)
    w_small = jnp.concatenate([w_in[:, o_kr:o_kr + MLA_ROPE], zeros(LANES - MLA_ROPE),
                               w_in[:, o_dt:o_cq], zeros(LANES - SSD_HEADS)], axis=1).astype(BF16)
    proj, small = norm_matmul(x, mix_g, w_main, w_small)

    y_ssd = ssd_mixer(proj, small, conv_w, conv_b.reshape(1, -1), _pad_lanes(dt_bias), _pad_lanes(a_log),
                      jnp.repeat(d_skip, SSD_HEAD_DIM).reshape(1, -1), ssd_norm.reshape(1, -1),
                      batch=batch, seq=seq, z_blk=0, xbc_blk=1, dt_blk=1)

    wq3 = w_uq.reshape(MLA_RANK, MLA_HEADS, MLA_NOPE + MLA_ROPE)
    wq_rope = jnp.pad(wq3[:, :, MLA_NOPE:], ((0, 0), (0, 0), (0, LANES - MLA_ROPE)))
    wq = jnp.concatenate([wq3[:, :, :MLA_NOPE].reshape(MLA_RANK, -1), wq_rope.reshape(MLA_RANK, -1)],
                         axis=1).astype(BF16)
    qn, qp, kv, kp = mla_prep(proj, small, q_norm.reshape(1, -1), kv_norm.reshape(1, -1), wq,
                              w_ukv.astype(BF16), *rope, seq=seq,
                              cq_blk=SSD_WIDTH // MLA_RANK, ckv_blk=SSD_WIDTH // MLA_RANK + 1, kr_blk=0)
    y_mla = mla_attention(qn, qp, kv, kp, batch=batch, seq=seq)

    w_out = w_out.astype(BF16)
    return matmul_residual([y_ssd, y_mla], [w_out[:SSD_WIDTH], w_out[SSD_WIDTH:]], x)


def _odd_layer(x, mix_g, w_qkv, w_out, *, batch, seq):
    width = SB_HEADS * SB_HEAD_DIM
    scale = SB_HEAD_DIM ** -0.5
    w = jnp.concatenate([w_qkv[:, :width] * scale, w_qkv[:, width:]], axis=1).astype(BF16)
    qkv = norm_matmul(x, mix_g, w)
    y = sb_attention(qkv, batch=batch, seq=seq)
    return matmul_residual([y], [w_out.astype(BF16)], x)


def _ffn(x, g, w_gate, w_up, w_down):
    hid = ffn_up(x, g, w_gate.astype(BF16), w_up.astype(BF16))
    return matmul_residual([hid], [w_down.astype(BF16)], x)


def kernel(x, mix_norm, ffn_norm, w_in, conv_w, conv_b, dt_bias, a_log, d_skip, ssd_norm, q_norm, kv_norm,
           w_uq, w_ukv, w_out_even, w_qkv, w_out_odd, w_gate, w_up, w_down, final_norm):
    batch, seq, d = x.shape
    depth = mix_norm.shape[0]
    rope = _rope_tables(seq)
    h = x.reshape(batch * seq, d)
    for layer in range(depth):
        mix_g = mix_norm[layer].reshape(1, -1)
        if layer % 2 == 0:
            e = layer // 2
            h = _even_layer(h, mix_g, w_in[e], conv_w[e], conv_b[e], dt_bias[e], a_log[e], d_skip[e],
                            ssd_norm[e], q_norm[e], kv_norm[e], w_uq[e], w_ukv[e], w_out_even[e], rope,
                            batch=batch, seq=seq)
        else:
            o = layer // 2
            h = _odd_layer(h, mix_g, w_qkv[o], w_out_odd[o], batch=batch, seq=seq)
        h = _ffn(h, ffn_norm[layer].reshape(1, -1), w_gate[layer], w_up[layer], w_down[layer])
    return final_rmsnorm(h, final_norm.reshape(1, -1)).reshape(batch, seq, d)
```

```python
import functools

import jax
import jax.numpy as jnp
import numpy as np
from jax import lax
from jax.experimental import pallas as pl
from jax.experimental.pallas import tpu as pltpu

F32 = jnp.float32
BF16 = jnp.bfloat16

RMS_EPS = 1e-6
ROPE_THETA = 10000.0

SSD_HEADS = 32
SSD_HEAD_DIM = 64
SSD_WIDTH = SSD_HEADS * SSD_HEAD_DIM
SSD_GROUPS = 4
SSD_STATE = 128
SSD_CONV = 4
SSD_CHUNK = 128
SSD_CONV_CH = SSD_WIDTH + 2 * SSD_GROUPS * SSD_STATE

MLA_HEADS = 16
MLA_RANK = 512
MLA_NOPE = 128
MLA_ROPE = 64
MLA_V = 128

SB_HEADS = 16
SB_HEAD_DIM = 128

LANES = 128
CONV_TAIL = 8
VMEM_LIMIT = 56 * 1024 * 1024
NCHUNK = 512
LOG2E = 1.4426950408889634
SB_DEAD_LOG2 = 151.0
NEG_BIG = -1e30

NT_DIMS = (((1,), (1,)), ((), ()))


def _cparams(*sem):
    return pltpu.CompilerParams(dimension_semantics=sem, vmem_limit_bytes=VMEM_LIMIT)


def _dot(a, b):
    return jnp.dot(a, b, preferred_element_type=F32)


def _dot_nt(a, b):
    return lax.dot_general(a, b, NT_DIMS, preferred_element_type=F32)


def _rms(x, g):
    ms = jnp.mean(x * x, axis=-1, keepdims=True)
    return x * lax.rsqrt(ms + RMS_EPS) * g


def _silu(x):
    return x / (1.0 + jnp.exp2(x * (-LOG2E)))


def _softplus(x):
    return jnp.maximum(x, 0.0) + jnp.log1p(jnp.exp(-jnp.abs(x)))


def _softplus_log2(z2):
    return jnp.maximum(z2, 0.0) + jnp.log(1.0 + jnp.exp2(-jnp.abs(z2))) * LOG2E


def _split3_dot(tri, x):
    x1 = x.astype(BF16)
    r1 = x - x1.astype(F32)
    x2 = r1.astype(BF16)
    x3 = (r1 - x2.astype(F32)).astype(BF16)
    return _dot(tri, x1) + _dot(tri, x2) + _dot(tri, x3)


def _norm_rows_into(x_ref, g_ref, h_scr, rows):
    tm = x_ref.shape[0]
    for r in range(0, tm, rows):
        x = x_ref[r:r + rows, :].astype(F32)
        h_scr[r:r + rows, :] = _rms(x, g_ref[...]).astype(BF16)


def _dot_cols_into(o_ref, a, w_ref, accumulate=False):
    n = w_ref.shape[1]
    step = min(NCHUNK, n)
    for c in range(0, n, step):
        r = _dot(a, w_ref[:, c:c + step])
        if accumulate:
            o_ref[:, c:c + step] += r
        else:
            o_ref[:, c:c + step] = r.astype(o_ref.dtype)


def _norm_mm_kernel(x_ref, g_ref, w_ref, o_ref, h_scr, *, rows):
    @pl.when(pl.program_id(1) == 0)
    def _():
        _norm_rows_into(x_ref, g_ref, h_scr, rows)

    _dot_cols_into(o_ref, h_scr[...], w_ref)


def _norm_mm2_kernel(x_ref, g_ref, w_ref, ws_ref, o_ref, os_ref, h_scr, *, rows):
    @pl.when(pl.program_id(1) == 0)
    def _():
        _norm_rows_into(x_ref, g_ref, h_scr, rows)
        os_ref[...] = _dot(h_scr[...], ws_ref[...])

    _dot_cols_into(o_ref, h_scr[...], w_ref)


def norm_matmul(x, g, w, w_small=None, *, tm=1024, tn=2048):
    t, k = x.shape
    n = w.shape[1]
    tm = min(tm, t)
    rows = min(256, tm)
    grid = (t // tm, n // tn)
    x_spec = pl.BlockSpec((tm, k), lambda i, j: (i, 0))
    g_spec = pl.BlockSpec((1, k), lambda i, j: (0, 0))
    w_spec = pl.BlockSpec((k, tn), lambda i, j: (0, j))
    o_spec = pl.BlockSpec((tm, tn), lambda i, j: (i, j))
    scratch = [pltpu.VMEM((tm, k), BF16)]
    if w_small is None:
        return pl.pallas_call(
            functools.partial(_norm_mm_kernel, rows=rows),
            grid=grid, in_specs=[x_spec, g_spec, w_spec], out_specs=o_spec,
            out_shape=jax.ShapeDtypeStruct((t, n), BF16), scratch_shapes=scratch,
            compiler_params=_cparams("parallel", "arbitrary"),
        )(x, g, w)
    ns = w_small.shape[1]
    return pl.pallas_call(
        functools.partial(_norm_mm2_kernel, rows=rows),
        grid=grid,
        in_specs=[x_spec, g_spec, w_spec, pl.BlockSpec((k, ns), lambda i, j: (0, 0))],
        out_specs=[o_spec, pl.BlockSpec((tm, ns), lambda i, j: (i, 0))],
        out_shape=[jax.ShapeDtypeStruct((t, n), BF16), jax.ShapeDtypeStruct((t, ns), F32)],
        scratch_shapes=scratch,
        compiler_params=_cparams("parallel", "arbitrary"),
    )(x, g, w, w_small)


def _cast_kernel(w_ref, o_ref):
    o_ref[...] = w_ref[...].astype(o_ref.dtype)


def layer_weight_bf16(w_stack, layer, *, tr=256):
    _, k, n = w_stack.shape
    tr = min(tr, k)
    return pl.pallas_call(
        _cast_kernel,
        grid=(k // tr,),
        in_specs=[pl.BlockSpec((None, tr, n), lambda i: (layer, i, 0))],
        out_specs=pl.BlockSpec((tr, n), lambda i: (i, 0)),
        out_shape=jax.ShapeDtypeStruct((k, n), BF16),
        compiler_params=_cparams("parallel"),
    )(w_stack)


def _ffn_kernel(*refs, rows, final):
    if final:
        x_ref, g_ref, wg_ref, wu_ref, wd_ref, fg_ref, o_ref, h_scr = refs
    else:
        x_ref, g_ref, wg_ref, wu_ref, wd_ref, o_ref, h_scr = refs
    c = pl.program_id(1)

    @pl.when(c == 0)
    def _():
        _norm_rows_into(x_ref, g_ref, h_scr, rows)
        o_ref[...] = x_ref[...]

    h = h_scr[...]
    hid = (_silu(_dot(h, wg_ref[...])) * _dot(h, wu_ref[...])).astype(BF16)
    _dot_cols_into(o_ref, hid, wd_ref, accumulate=True)

    if final:
        @pl.when(c == pl.num_programs(1) - 1)
        def _():
            for r in range(0, o_ref.shape[0], rows):
                o_ref[r:r + rows, :] = _rms(o_ref[r:r + rows, :], fg_ref[...])


def ffn(x, g, wg, wu, wd, final_g=None, *, tm=1024, th=512):
    t, k = x.shape
    hidden = wg.shape[1]
    tm = min(tm, t)
    rows = min(256, tm)
    final = final_g is not None
    const = lambda i, c: (0, 0)
    in_specs = [pl.BlockSpec((tm, k), lambda i, c: (i, 0), pipeline_mode=pl.Buffered(1)),
                pl.BlockSpec((1, k), const),
                pl.BlockSpec((k, th), lambda i, c: (0, c)),
                pl.BlockSpec((k, th), lambda i, c: (0, c)),
                pl.BlockSpec((th, k), lambda i, c: (c, 0))]
    args = [x, g, wg, wu, wd]
    if final:
        in_specs.append(pl.BlockSpec((1, k), const))
        args.append(final_g)
    return pl.pallas_call(
        functools.partial(_ffn_kernel, rows=rows, final=final),
        grid=(t // tm, hidden // th),
        in_specs=in_specs,
        out_specs=pl.BlockSpec((tm, k), lambda i, c: (i, 0)),
        out_shape=jax.ShapeDtypeStruct((t, k), F32),
        scratch_shapes=[pltpu.VMEM((tm, k), BF16)],
        compiler_params=_cparams("parallel", "arbitrary"),
    )(*args)


def _mm_res_kernel(*refs, n_pairs):
    a_refs = refs[:n_pairs]
    w_refs = refs[n_pairs:2 * n_pairs]
    r_ref, o_ref = refs[2 * n_pairs], refs[2 * n_pairs + 1]
    n = o_ref.shape[1]
    for c in range(0, n, NCHUNK):
        acc = r_ref[:, c:c + NCHUNK]
        for a_ref, w_ref in zip(a_refs, w_refs):
            acc = acc + _dot(a_ref[...], w_ref[:, c:c + NCHUNK])
        o_ref[:, c:c + NCHUNK] = acc


def matmul_residual(a_list, w_list, res, *, tm=512):
    t, n = res.shape
    tm = min(tm, t)
    n_pairs = len(a_list)
    in_specs = [pl.BlockSpec((tm, a.shape[1]), lambda i: (i, 0)) for a in a_list]
    in_specs += [pl.BlockSpec(w.shape, lambda i: (0, 0)) for w in w_list]
    in_specs += [pl.BlockSpec((tm, n), lambda i: (i, 0))]
    return pl.pallas_call(
        functools.partial(_mm_res_kernel, n_pairs=n_pairs),
        grid=(t // tm,),
        in_specs=in_specs,
        out_specs=pl.BlockSpec((tm, n), lambda i: (i, 0)),
        out_shape=jax.ShapeDtypeStruct((t, n), F32),
        compiler_params=_cparams("parallel"),
    )(*a_list, *w_list, res)


def _pair_cols(arr, h0, lane_lo):
    rows = arr.shape[0]
    c0 = jnp.broadcast_to(arr[:, h0:h0 + 1], (rows, LANES))
    c1 = jnp.broadcast_to(arr[:, h0 + 1:h0 + 2], (rows, LANES))
    return jnp.where(lane_lo, c0, c1)


def _ssd_kernel(z_ref, xbc_ref, dt_ref, cw_ref, cb_ref, dtb_ref, alog_ref, dsk_ref, g_ref, tril_ref,
                o_ref, xext, xc, state, y_scr):
    L = SSD_CHUNK
    hp = SSD_HEAD_DIM
    gw = (SSD_HEADS // SSD_GROUPS) * hp
    b_off = SSD_WIDTH
    c_off = SSD_WIDTH + SSD_GROUPS * SSD_STATE

    @pl.when(pl.program_id(1) == 0)
    def _():
        xext[0:CONV_TAIL, :] = jnp.zeros((CONV_TAIL, SSD_CONV_CH), F32)
        state[...] = jnp.zeros_like(state)

    xext[CONV_TAIL:CONV_TAIL + L, :] = xbc_ref[...].astype(F32)
    cblk = 512
    for c0 in range(0, SSD_CONV_CH, cblk):
        cs = slice(c0, c0 + cblk)
        acc = cb_ref[:, cs] + cw_ref[0:1, cs] * xext[CONV_TAIL - 3:CONV_TAIL - 3 + L, cs]
        for k in range(1, SSD_CONV):
            r0 = CONV_TAIL - 3 + k
            acc = acc + cw_ref[k:k + 1, cs] * xext[r0:r0 + L, cs]
        xc[:, cs] = _silu(acc)
    xext[0:CONV_TAIL, :] = xext[L:L + CONV_TAIL, :]

    dt = _softplus(dt_ref[...] + dtb_ref[...])
    a = jnp.exp(alog_ref[...]) * (-LOG2E)
    a_cs = _split3_dot(tril_ref[...], dt * a)
    a_cs_t = a_cs.T
    a_end = a_cs[L - 1:L, :]
    exp_a = jnp.exp2(a_cs)
    dt_to_end = dt * jnp.exp2(a_end - a_cs)
    chunk_decay = jnp.exp2(a_end)

    row = lax.broadcasted_iota(jnp.int32, (L, L), 0)
    col = lax.broadcasted_iota(jnp.int32, (L, L), 1)
    causal = col <= row
    lane_lo = lax.broadcasted_iota(jnp.int32, (L, LANES), 1) < hp
    lane_lo_row = lax.broadcasted_iota(jnp.int32, (1, LANES), 1) < hp

    for g in range(SSD_GROUPS):
        b_g = xc[:, b_off + g * SSD_STATE:b_off + (g + 1) * SSD_STATE]
        c_g = xc[:, c_off + g * SSD_STATE:c_off + (g + 1) * SSD_STATE].astype(BF16)
        cb = _dot_nt(c_g, b_g.astype(BF16))
        b_g_t = b_g.T.astype(BF16)
        y_off = _dot(c_g, state[:, g * gw:(g + 1) * gw].astype(BF16))
        for k in range(gw // LANES):
            h0 = g * (SSD_HEADS // SSD_GROUPS) + 2 * k
            cs = slice(g * gw + k * LANES, g * gw + (k + 1) * LANES)
            x_pair = xc[:, cs]
            xdt = (x_pair * _pair_cols(dt, h0, lane_lo)).astype(BF16)
            ys = []
            for h in (h0, h0 + 1):
                seg = jnp.broadcast_to(a_cs[:, h:h + 1], (L, L)) - jnp.broadcast_to(a_cs_t[h:h + 1, :], (L, L))
                decay = jnp.exp2(jnp.where(causal, seg, NEG_BIG))
                ys.append(_dot((cb * decay).astype(BF16), xdt))
            y = jnp.where(lane_lo, ys[0], ys[1])
            y = y + y_off[:, k * LANES:(k + 1) * LANES] * _pair_cols(exp_a, h0, lane_lo)
            y_scr[:, cs] = y + dsk_ref[:, cs] * x_pair
            x_to_end = (x_pair * _pair_cols(dt_to_end, h0, lane_lo)).astype(BF16)
            cd = jnp.where(lane_lo_row,
                           jnp.broadcast_to(chunk_decay[:, h0:h0 + 1], (1, LANES)),
                           jnp.broadcast_to(chunk_decay[:, h0 + 1:h0 + 2], (1, LANES)))
            state[:, cs] = state[:, cs] * cd + _dot(b_g_t, x_to_end)

    zz = z_ref[...].astype(F32)
    o_ref[...] = _rms(y_scr[...] * _silu(zz), g_ref[...]).astype(o_ref.dtype)


def ssd_mixer(proj, small, conv_w, conv_b, dt_bias_pad, a_log_pad, d_skip_rep, norm_g, *, batch, seq,
              z_blk, xbc_blk, dt_blk):
    L = SSD_CHUNK
    nc = seq // L
    tril = jnp.asarray(np.tril(np.ones((L, L), np.float32)), BF16)
    const = lambda b, c: (0, 0)
    return pl.pallas_call(
        _ssd_kernel,
        grid=(batch, nc),
        in_specs=[pl.BlockSpec((L, SSD_WIDTH), lambda b, c: (b * nc + c, z_blk)),
                  pl.BlockSpec((L, SSD_CONV_CH), lambda b, c: (b * nc + c, xbc_blk)),
                  pl.BlockSpec((L, LANES), lambda b, c: (b * nc + c, dt_blk)),
                  pl.BlockSpec((SSD_CONV, SSD_CONV_CH), const),
                  pl.BlockSpec((1, SSD_CONV_CH), const),
                  pl.BlockSpec((1, LANES), const),
                  pl.BlockSpec((1, LANES), const),
                  pl.BlockSpec((1, SSD_WIDTH), const),
                  pl.BlockSpec((1, SSD_WIDTH), const),
                  pl.BlockSpec((L, L), const)],
        out_specs=pl.BlockSpec((L, SSD_WIDTH), lambda b, c: (b * nc + c, 0)),
        out_shape=jax.ShapeDtypeStruct((batch * seq, SSD_WIDTH), BF16),
        scratch_shapes=[pltpu.VMEM((CONV_TAIL + L, SSD_CONV_CH), F32),
                        pltpu.VMEM((L, SSD_CONV_CH), F32),
                        pltpu.VMEM((SSD_STATE, SSD_WIDTH), F32),
                        pltpu.VMEM((L, SSD_WIDTH), F32)],
        compiler_params=_cparams("parallel", "arbitrary"),
    )(proj, proj, small, conv_w, conv_b, dt_bias_pad, a_log_pad, d_skip_rep, norm_g, tril)


def _rope(x, c, sa, sb):
    half = MLA_ROPE // 2
    return x * c + pltpu.roll(x, LANES - half, 1) * sa + pltpu.roll(x, half, 1) * sb


def _mla_prep_kernel(cq_ref, ckv_ref, kr_ref, qg_ref, kvg_ref, wq_ref, wkv_ref, c_ref, sa_ref, sb_ref,
                     qt_ref, kn_ref, vt_ref, kp_ref):
    scale = float((MLA_NOPE + MLA_ROPE) ** -0.5) * LOG2E
    hq = _rms(cq_ref[...].astype(F32), qg_ref[...]).astype(BF16)
    hkv = _rms(ckv_ref[...].astype(F32), kvg_ref[...]).astype(BF16)
    c, sa, sb = c_ref[...], sa_ref[...], sb_ref[...]
    nope_w = MLA_HEADS * MLA_NOPE
    qd = MLA_NOPE + LANES
    nblk = 512
    for n in range(0, nope_w, nblk):
        qn = _dot(hq, wq_ref[:, n:n + nblk]) * scale
        qp = _dot(hq, wq_ref[:, nope_w + n:nope_w + n + nblk])
        for s in range(0, nblk, LANES):
            h = (n + s) // LANES
            qt_ref[h * qd:h * qd + MLA_NOPE, :] = qn[:, s:s + LANES].T.astype(BF16)
            qt_ref[h * qd + MLA_NOPE:(h + 1) * qd, :] = (
                _rope(qp[:, s:s + LANES], c, sa, sb) * scale).T.astype(BF16)
    for n in range(0, nope_w, nblk):
        kn_ref[:, n:n + nblk] = _dot(hkv, wkv_ref[:, n:n + nblk]).astype(BF16)
    for n in range(0, MLA_HEADS * MLA_V, nblk):
        v = _dot(hkv, wkv_ref[:, nope_w + n:nope_w + n + nblk])
        for s in range(0, nblk, MLA_V):
            vt_ref[n + s:n + s + MLA_V, :] = v[:, s:s + MLA_V].T.astype(BF16)
    kp_ref[...] = _rope(kr_ref[...], c, sa, sb).astype(BF16)


def mla_prep(proj, small, q_norm, kv_norm, wq, wkv, rope_c, rope_sa, rope_sb, *, seq, cq_blk, ckv_blk,
             kr_blk, tm):
    t = proj.shape[0]
    npos = seq // tm
    const = lambda i: (0, 0)
    pos = lambda i: (i % npos, 0)
    nt = t // tm
    qtw = MLA_HEADS * (MLA_NOPE + LANES)
    knw = MLA_HEADS * MLA_NOPE
    vw = MLA_HEADS * MLA_V
    return pl.pallas_call(
        _mla_prep_kernel,
        grid=(t // tm,),
        in_specs=[pl.BlockSpec((tm, MLA_RANK), lambda i: (i, cq_blk)),
                  pl.BlockSpec((tm, MLA_RANK), lambda i: (i, ckv_blk)),
                  pl.BlockSpec((tm, LANES), lambda i: (i, kr_blk)),
                  pl.BlockSpec((1, MLA_RANK), const),
                  pl.BlockSpec((1, MLA_RANK), const),
                  pl.BlockSpec(wq.shape, const),
                  pl.BlockSpec(wkv.shape, const),
                  pl.BlockSpec((tm, LANES), pos),
                  pl.BlockSpec((tm, LANES), pos),
                  pl.BlockSpec((tm, LANES), pos)],
        out_specs=[pl.BlockSpec((None, qtw, tm), lambda i: (i, 0, 0)),
                   pl.BlockSpec((tm, knw), lambda i: (i, 0)),
                   pl.BlockSpec((None, vw, tm), lambda i: (i, 0, 0)),
                   pl.BlockSpec((tm, LANES), lambda i: (i, 0))],
        out_shape=[jax.ShapeDtypeStruct((nt, qtw, tm), BF16),
                   jax.ShapeDtypeStruct((t, knw), BF16),
                   jax.ShapeDtypeStruct((nt, vw, tm), BF16),
                   jax.ShapeDtypeStruct((t, LANES), BF16)],
        compiler_params=_cparams("parallel"),
    )(proj, proj, small, q_norm, kv_norm, wq, wkv, rope_c, rope_sa, rope_sb)


MLA_HEADS_PER_STEP = 4
MLA_TILE = 512
MLA_ONES_ROWS = 16


def _mla_attn_kernel(qt_ref, kn_ref, kp_ref, vt_ref, o_ref, *, tq):
    i = pl.program_id(2)
    heads = range(MLA_HEADS_PER_STEP)
    qd = MLA_NOPE + LANES
    ones = jnp.ones((MLA_ONES_ROWS, tq), BF16)

    def block(kb, carry, diagonal):
        ks = pl.multiple_of(kb * tq, tq)
        kp = kp_ref[pl.ds(ks, tq), :]
        sts = [_dot(jnp.concatenate([kn_ref[pl.ds(ks, tq), h * MLA_NOPE:(h + 1) * MLA_NOPE], kp], axis=1),
                    qt_ref[h * qd:(h + 1) * qd, :]) for h in heads]
        if diagonal:
            key = lax.broadcasted_iota(jnp.int32, (tq, tq), 0)
            qry = lax.broadcasted_iota(jnp.int32, (tq, tq), 1)
            sts = [jnp.where(key <= qry, st, NEG_BIG) for st in sts]
        stats = []
        for h in heads:
            m = carry[h][0]
            m_new = jnp.maximum(m, jnp.max(sts[h], axis=0, keepdims=True))
            stats.append((m_new, jnp.exp2(m - m_new), jnp.exp2(sts[h] - m_new).astype(BF16)))
        out = []
        for h in heads:
            m_new, alpha, pt = stats[h]
            vt1 = jnp.concatenate([vt_ref[kb, h * MLA_V:(h + 1) * MLA_V, :], ones], axis=0)
            out.append((m_new, carry[h][1] * alpha + _dot(vt1, pt)))
        return tuple(out)

    init = tuple((jnp.full((1, tq), NEG_BIG, F32), jnp.zeros((MLA_V + MLA_ONES_ROWS, tq), F32)) for _ in heads)
    carry = lax.fori_loop(0, i, lambda kb, c: block(kb, c, False), init)
    carry = block(i, carry, True)
    for h in heads:
        acc = carry[h][1]
        out_t = acc[:MLA_V, :] / acc[MLA_V:MLA_V + 1, :]
        o_ref[:, h * MLA_V:(h + 1) * MLA_V] = out_t.T.astype(o_ref.dtype)


def mla_attention(qt, kn, kp, vt, *, batch, seq, tq):
    nq = seq // tq
    t = batch * seq
    nh = MLA_HEADS_PER_STEP
    qd = MLA_NOPE + LANES
    return pl.pallas_call(
        functools.partial(_mla_attn_kernel, tq=tq),
        grid=(batch, MLA_HEADS // nh, nq),
        in_specs=[pl.BlockSpec((None, nh * qd, tq), lambda b, h, i: (b * nq + i, h, 0)),
                  pl.BlockSpec((seq, nh * MLA_NOPE), lambda b, h, i: (b, h)),
                  pl.BlockSpec((seq, LANES), lambda b, h, i: (b, 0)),
                  pl.BlockSpec((nq, nh * MLA_V, tq), lambda b, h, i: (b, h, 0))],
        out_specs=pl.BlockSpec((tq, nh * MLA_V), lambda b, h, i: (b * nq + i, h)),
        out_shape=jax.ShapeDtypeStruct((t, MLA_HEADS * MLA_V), BF16),
        compiler_params=_cparams("parallel", "parallel", "arbitrary"),
    )(qt, kn, kp, vt)


SB_HEADS_PER_STEP = 4


def _sb_attn_kernel(q_ref, k_ref, v_ref, u_ref, o_ref, *, tq):
    i = pl.program_id(2)
    u = u_ref[...]
    hd = SB_HEAD_DIM
    heads = range(SB_HEADS_PER_STEP)

    def block(kb, carries, accs, masked):
        ks = pl.multiple_of(kb * tq, tq)
        if masked:
            row = lax.broadcasted_iota(jnp.int32, (tq, tq), 0)
            col = lax.broadcasted_iota(jnp.int32, (tq, tq), 1)
            keep = col < row
        cols = [slice(h * hd, (h + 1) * hd) for h in heads]
        zs = [_dot_nt(q_ref[:, c], k_ref[pl.ds(ks, tq), c]) for c in cols]
        sps = [_softplus_log2(z) for z in zs]
        if masked:
            sps = [jnp.where(keep, sp, 0.0) for sp in sps]
        his = [sp.astype(BF16) for sp in sps]
        los = [(sp - hi.astype(F32)).astype(BF16) for sp, hi in zip(sps, his)]
        rsums = [_dot(hi, u) + _dot(lo, u) for hi, lo in zip(his, los)]
        ws = []
        for h in heads:
            carry_b = jnp.concatenate([carries[h]] * (tq // LANES), axis=1)
            w = jnp.exp2(zs[h] - rsums[h] - carry_b)
            ws.append(jnp.where(keep, w, 0.0) if masked else w)
        new_accs = [accs[h] + _dot(ws[h].astype(BF16), v_ref[pl.ds(ks, tq), cols[h]]) for h in heads]
        new_carries = [carries[h] + jnp.broadcast_to(rsums[h][:, 0:1], (tq, LANES)) for h in heads]
        return tuple(new_carries), tuple(new_accs)

    def live(carries):
        return jnp.min(functools.reduce(jnp.minimum, carries))

    zeros = tuple(jnp.zeros((tq, LANES), F32) for _ in heads)
    carries, accs = block(i, zeros, zeros, True)

    def cond(st):
        kb, least, _, _ = st
        return jnp.logical_and(kb >= 0, least < SB_DEAD_LOG2)

    def body(st):
        kb, _, carries, accs = st
        carries, accs = block(kb, carries, accs, False)
        return kb - 1, live(carries), carries, accs

    _, _, _, accs = lax.while_loop(cond, body, (i - 1, live(carries), carries, accs))
    for h in heads:
        o_ref[:, h * hd:(h + 1) * hd] = accs[h].astype(o_ref.dtype)


def sb_attention(qkv, *, batch, seq, tq=256):
    tq = min(tq, seq)
    nq = seq // tq
    t = batch * seq
    w = SB_HEADS_PER_STEP * SB_HEAD_DIM
    nhb = SB_HEADS // SB_HEADS_PER_STEP
    u = jnp.asarray(np.tril(np.ones((tq, tq), np.float32)), BF16)
    return pl.pallas_call(
        functools.partial(_sb_attn_kernel, tq=tq),
        grid=(batch, nhb, nq),
        in_specs=[pl.BlockSpec((tq, w), lambda b, h, i: (b * nq + i, h)),
                  pl.BlockSpec((seq, w), lambda b, h, i: (b, nhb + h)),
                  pl.BlockSpec((seq, w), lambda b, h, i: (b, 2 * nhb + h)),
                  pl.BlockSpec((tq, tq), lambda b, h, i: (0, 0))],
        out_specs=pl.BlockSpec((tq, w), lambda b, h, i: (b * nq + i, h)),
        out_shape=jax.ShapeDtypeStruct((t, SB_HEADS * SB_HEAD_DIM), BF16),
        compiler_params=_cparams("parallel", "parallel", "arbitrary"),
    )(qkv, qkv, qkv, u)


def _rope_tables(seq):
    half = MLA_ROPE // 2
    inv_freq = ROPE_THETA ** (-jnp.arange(half, dtype=F32) / half)
    ang = jnp.arange(seq, dtype=F32)[:, None] * inv_freq[None, :]
    cos, sin = jnp.cos(ang), jnp.sin(ang)
    zero = jnp.zeros_like(cos)
    pad = jnp.zeros((seq, LANES - MLA_ROPE), F32)
    c = jnp.concatenate([cos, cos, pad], axis=1)
    sa = jnp.concatenate([-sin, zero, pad], axis=1)
    sb = jnp.concatenate([zero, sin, pad], axis=1)
    return c, sa, sb


def _pad_lanes(v, width=LANES):
    v = v.reshape(1, -1).astype(F32)
    return jnp.pad(v, ((0, 0), (0, width - v.shape[1])))


def _even_layer(x, mix_g, w_in, conv_w, conv_b, dt_bias, a_log, d_skip, ssd_norm, q_norm, kv_norm,
                w_uq, w_ukv, w_out, rope, *, batch, seq):
    d = x.shape[1]
    o_z, o_xbc, o_dt, o_cq, o_ckv, o_kr = np.cumsum(
        [0, SSD_WIDTH, SSD_CONV_CH, SSD_HEADS, MLA_RANK, MLA_RANK]).tolist()
    w_main = jnp.concatenate([w_in[:, o_z:o_xbc], w_in[:, o_cq:o_ckv], w_in[:, o_ckv:o_kr],
                              w_in[:, o_xbc:o_dt]], axis=1).astype(BF16)
    zeros = lambda n: jnp.zeros((d, n), F32)
    w_small = jnp.concatenate([w_in[:, o_kr:o_kr + MLA_ROPE], zeros(LANES - MLA_ROPE),
                               w_in[:, o_dt:o_cq], zeros(LANES - SSD_HEADS)], axis=1).astype(BF16)
    proj, small = norm_matmul(x, mix_g, w_main, w_small)

    y_ssd = ssd_mixer(proj, small, conv_w, conv_b.reshape(1, -1), _pad_lanes(dt_bias), _pad_lanes(a_log),
                      jnp.repeat(d_skip, SSD_HEAD_DIM).reshape(1, -1), ssd_norm.reshape(1, -1),
                      batch=batch, seq=seq, z_blk=0, xbc_blk=1, dt_blk=1)

    wq3 = w_uq.reshape(MLA_RANK, MLA_HEADS, MLA_NOPE + MLA_ROPE)
    wq_rope = jnp.pad(wq3[:, :, MLA_NOPE:], ((0, 0), (0, 0), (0, LANES - MLA_ROPE)))
    wq = jnp.concatenate([wq3[:, :, :MLA_NOPE].reshape(MLA_RANK, -1), wq_rope.reshape(MLA_RANK, -1)],
                         axis=1).astype(BF16)
    wkv3 = w_ukv.reshape(MLA_RANK, MLA_HEADS, MLA_NOPE + MLA_V)
    wkv = jnp.concatenate([wkv3[:, :, :MLA_NOPE].reshape(MLA_RANK, -1),
                           wkv3[:, :, MLA_NOPE:].reshape(MLA_RANK, -1)], axis=1).astype(BF16)
    tq = min(MLA_TILE, seq)
    qt, kn, vt, kp = mla_prep(proj, small, q_norm.reshape(1, -1), kv_norm.reshape(1, -1), wq, wkv, *rope,
                              seq=seq, cq_blk=SSD_WIDTH // MLA_RANK, ckv_blk=SSD_WIDTH // MLA_RANK + 1,
                              kr_blk=0, tm=tq)
    y_mla = mla_attention(qt, kn, kp, vt, batch=batch, seq=seq, tq=tq)

    w_out = w_out.astype(BF16)
    return matmul_residual([y_ssd, y_mla], [w_out[:SSD_WIDTH], w_out[SSD_WIDTH:]], x)


def _odd_layer(x, mix_g, w_qkv, w_out, *, batch, seq):
    width = SB_HEADS * SB_HEAD_DIM
    scale = SB_HEAD_DIM ** -0.5 * LOG2E
    w = jnp.concatenate([w_qkv[:, :width] * scale, w_qkv[:, width:]], axis=1).astype(BF16)
    qkv = norm_matmul(x, mix_g, w)
    y = sb_attention(qkv, batch=batch, seq=seq)
    return matmul_residual([y], [w_out.astype(BF16)], x)


def _ffn(x, g, w_gate, w_up, w_down, layer, final_g=None):
    wg, wu, wd = (layer_weight_bf16(w, layer) for w in (w_gate, w_up, w_down))
    return ffn(x, g, wg, wu, wd, final_g)


def kernel(x, mix_norm, ffn_norm, w_in, conv_w, conv_b, dt_bias, a_log, d_skip, ssd_norm, q_norm, kv_norm,
           w_uq, w_ukv, w_out_even, w_qkv, w_out_odd, w_gate, w_up, w_down, final_norm):
    batch, seq, d = x.shape
    depth = mix_norm.shape[0]
    rope = _rope_tables(seq)
    h = x.reshape(batch * seq, d)
    for layer in range(depth):
        mix_g = mix_norm[layer].reshape(1, -1)
        if layer % 2 == 0:
            e = layer // 2
            h = _even_layer(h, mix_g, w_in[e], conv_w[e], conv_b[e], dt_bias[e], a_log[e], d_skip[e],
                            ssd_norm[e], q_norm[e], kv_norm[e], w_uq[e], w_ukv[e], w_out_even[e], rope,
                            batch=batch, seq=seq)
        else:
            o = layer // 2
            h = _odd_layer(h, mix_g, w_qkv[o], w_out_odd[o], batch=batch, seq=seq)
        final_g = final_norm.reshape(1, -1) if layer == depth - 1 else None
        h = _ffn(h, ffn_norm[layer].reshape(1, -1), w_gate, w_up, w_down, layer, final_g)
    return h.reshape(batch, seq, d)
```

```python
import functools

import jax
import jax.numpy as jnp
import numpy as np
from jax import lax
from jax.experimental import pallas as pl
from jax.experimental.pallas import tpu as pltpu

F32 = jnp.float32
BF16 = jnp.bfloat16

RMS_EPS = 1e-6
ROPE_THETA = 10000.0

SSD_HEADS = 32
SSD_HEAD_DIM = 64
SSD_WIDTH = SSD_HEADS * SSD_HEAD_DIM
SSD_GROUPS = 4
SSD_STATE = 128
SSD_CONV = 4
SSD_CHUNK = 128
SSD_CONV_CH = SSD_WIDTH + 2 * SSD_GROUPS * SSD_STATE

MLA_HEADS = 16
MLA_RANK = 512
MLA_NOPE = 128
MLA_ROPE = 64
MLA_V = 128

SB_HEADS = 16
SB_HEAD_DIM = 128

LANES = 128
CONV_TAIL = 8
VMEM_LIMIT = 56 * 1024 * 1024
NCHUNK = 512
LOG2E = 1.4426950408889634
SB_DEAD_LOG2 = 151.0
NEG_BIG = -1e30

NT_DIMS = (((1,), (1,)), ((), ()))


def _cparams(*sem):
    return pltpu.CompilerParams(dimension_semantics=sem, vmem_limit_bytes=VMEM_LIMIT)


def _dot(a, b):
    return jnp.dot(a, b, preferred_element_type=F32)


def _dot_nt(a, b):
    return lax.dot_general(a, b, NT_DIMS, preferred_element_type=F32)


def _rms(x, g):
    ms = jnp.mean(x * x, axis=-1, keepdims=True)
    return x * lax.rsqrt(ms + RMS_EPS) * g


def _silu(x):
    return x / (1.0 + jnp.exp2(x * (-LOG2E)))


def _softplus(x):
    return jnp.maximum(x, 0.0) + jnp.log1p(jnp.exp(-jnp.abs(x)))


def _softplus_log2(z2):
    return jnp.maximum(z2, 0.0) + jnp.log(1.0 + jnp.exp2(-jnp.abs(z2))) * LOG2E


def _split3_dot(tri, x):
    x1 = x.astype(BF16)
    r1 = x - x1.astype(F32)
    x2 = r1.astype(BF16)
    x3 = (r1 - x2.astype(F32)).astype(BF16)
    return _dot(tri, x1) + _dot(tri, x2) + _dot(tri, x3)


def _norm_rows_into(x_ref, g_ref, h_scr, rows):
    tm = x_ref.shape[0]
    for r in range(0, tm, rows):
        x = x_ref[r:r + rows, :].astype(F32)
        h_scr[r:r + rows, :] = _rms(x, g_ref[...]).astype(BF16)


def _dot_cols_into(o_ref, a, w_ref, accumulate=False):
    n = w_ref.shape[1]
    step = min(NCHUNK, n)
    for c in range(0, n, step):
        r = _dot(a, w_ref[:, c:c + step])
        if accumulate:
            o_ref[:, c:c + step] += r
        else:
            o_ref[:, c:c + step] = r.astype(o_ref.dtype)


def _norm_mm_kernel(x_ref, g_ref, w_ref, o_ref, h_scr, *, rows):
    @pl.when(pl.program_id(1) == 0)
    def _():
        _norm_rows_into(x_ref, g_ref, h_scr, rows)

    _dot_cols_into(o_ref, h_scr[...], w_ref)


def _norm_mm2_kernel(x_ref, g_ref, w_ref, ws_ref, o_ref, os_ref, h_scr, *, rows):
    @pl.when(pl.program_id(1) == 0)
    def _():
        _norm_rows_into(x_ref, g_ref, h_scr, rows)
        os_ref[...] = _dot(h_scr[...], ws_ref[...])

    _dot_cols_into(o_ref, h_scr[...], w_ref)


def norm_matmul(x, g, w, w_small=None, *, tm=1024, tn=2048):
    t, k = x.shape
    n = w.shape[1]
    tm = min(tm, t)
    rows = min(256, tm)
    grid = (t // tm, n // tn)
    x_spec = pl.BlockSpec((tm, k), lambda i, j: (i, 0))
    g_spec = pl.BlockSpec((1, k), lambda i, j: (0, 0))
    w_spec = pl.BlockSpec((k, tn), lambda i, j: (0, j))
    o_spec = pl.BlockSpec((tm, tn), lambda i, j: (i, j))
    scratch = [pltpu.VMEM((tm, k), BF16)]
    if w_small is None:
        return pl.pallas_call(
            functools.partial(_norm_mm_kernel, rows=rows),
            grid=grid, in_specs=[x_spec, g_spec, w_spec], out_specs=o_spec,
            out_shape=jax.ShapeDtypeStruct((t, n), BF16), scratch_shapes=scratch,
            compiler_params=_cparams("parallel", "arbitrary"),
        )(x, g, w)
    ns = w_small.shape[1]
    return pl.pallas_call(
        functools.partial(_norm_mm2_kernel, rows=rows),
        grid=grid,
        in_specs=[x_spec, g_spec, w_spec, pl.BlockSpec((k, ns), lambda i, j: (0, 0))],
        out_specs=[o_spec, pl.BlockSpec((tm, ns), lambda i, j: (i, 0))],
        out_shape=[jax.ShapeDtypeStruct((t, n), BF16), jax.ShapeDtypeStruct((t, ns), F32)],
        scratch_shapes=scratch,
        compiler_params=_cparams("parallel", "arbitrary"),
    )(x, g, w, w_small)


def _cast_kernel(w_ref, o_ref):
    o_ref[...] = w_ref[...].astype(o_ref.dtype)


def layer_weight_bf16(w_stack, layer, *, tr=256):
    _, k, n = w_stack.shape
    tr = min(tr, k)
    return pl.pallas_call(
        _cast_kernel,
        grid=(k // tr,),
        in_specs=[pl.BlockSpec((None, tr, n), lambda i: (layer, i, 0))],
        out_specs=pl.BlockSpec((tr, n), lambda i: (i, 0)),
        out_shape=jax.ShapeDtypeStruct((k, n), BF16),
        compiler_params=_cparams("parallel"),
    )(w_stack)


def _ffn_kernel(*refs, rows, final):
    if final:
        x_ref, g_ref, wg_ref, wu_ref, wd_ref, fg_ref, o_ref, h_scr = refs
    else:
        x_ref, g_ref, wg_ref, wu_ref, wd_ref, o_ref, h_scr = refs
    c = pl.program_id(1)

    @pl.when(c == 0)
    def _():
        _norm_rows_into(x_ref, g_ref, h_scr, rows)
        o_ref[...] = x_ref[...]

    h = h_scr[...]
    hid = (_silu(_dot(h, wg_ref[...])) * _dot(h, wu_ref[...])).astype(BF16)
    _dot_cols_into(o_ref, hid, wd_ref, accumulate=True)

    if final:
        @pl.when(c == pl.num_programs(1) - 1)
        def _():
            for r in range(0, o_ref.shape[0], rows):
                o_ref[r:r + rows, :] = _rms(o_ref[r:r + rows, :], fg_ref[...])


def ffn(x, g, wg, wu, wd, final_g=None, *, tm=1024, th=512):
    t, k = x.shape
    hidden = wg.shape[1]
    tm = min(tm, t)
    rows = min(256, tm)
    final = final_g is not None
    const = lambda i, c: (0, 0)
    in_specs = [pl.BlockSpec((tm, k), lambda i, c: (i, 0), pipeline_mode=pl.Buffered(1)),
                pl.BlockSpec((1, k), const),
                pl.BlockSpec((k, th), lambda i, c: (0, c)),
                pl.BlockSpec((k, th), lambda i, c: (0, c)),
                pl.BlockSpec((th, k), lambda i, c: (c, 0))]
    args = [x, g, wg, wu, wd]
    if final:
        in_specs.append(pl.BlockSpec((1, k), const))
        args.append(final_g)
    return pl.pallas_call(
        functools.partial(_ffn_kernel, rows=rows, final=final),
        grid=(t // tm, hidden // th),
        in_specs=in_specs,
        out_specs=pl.BlockSpec((tm, k), lambda i, c: (i, 0)),
        out_shape=jax.ShapeDtypeStruct((t, k), F32),
        scratch_shapes=[pltpu.VMEM((tm, k), BF16)],
        compiler_params=_cparams("parallel", "arbitrary"),
    )(*args)


def _mm_res_kernel(*refs, n_pairs):
    a_refs = refs[:n_pairs]
    w_refs = refs[n_pairs:2 * n_pairs]
    r_ref, o_ref = refs[2 * n_pairs], refs[2 * n_pairs + 1]
    n = o_ref.shape[1]
    for c in range(0, n, NCHUNK):
        acc = r_ref[:, c:c + NCHUNK]
        for a_ref, w_ref in zip(a_refs, w_refs):
            acc = acc + _dot(a_ref[...], w_ref[:, c:c + NCHUNK])
        o_ref[:, c:c + NCHUNK] = acc


def matmul_residual(a_list, w_list, res, *, tm=512):
    t, n = res.shape
    tm = min(tm, t)
    n_pairs = len(a_list)
    in_specs = [pl.BlockSpec((tm, a.shape[1]), lambda i: (i, 0)) for a in a_list]
    in_specs += [pl.BlockSpec(w.shape, lambda i: (0, 0)) for w in w_list]
    in_specs += [pl.BlockSpec((tm, n), lambda i: (i, 0))]
    return pl.pallas_call(
        functools.partial(_mm_res_kernel, n_pairs=n_pairs),
        grid=(t // tm,),
        in_specs=in_specs,
        out_specs=pl.BlockSpec((tm, n), lambda i: (i, 0)),
        out_shape=jax.ShapeDtypeStruct((t, n), F32),
        compiler_params=_cparams("parallel"),
    )(*a_list, *w_list, res)


def _pair_cols(arr, h0, lane_lo):
    rows = arr.shape[0]
    c0 = jnp.broadcast_to(arr[:, h0:h0 + 1], (rows, LANES))
    c1 = jnp.broadcast_to(arr[:, h0 + 1:h0 + 2], (rows, LANES))
    return jnp.where(lane_lo, c0, c1)


def _ssd_kernel(z_ref, xbc_ref, dt_ref, cw_ref, cb_ref, dtb_ref, alog_ref, dsk_ref, g_ref, tril_ref,
                o_ref, xext, xc, state, y_scr):
    L = SSD_CHUNK
    hp = SSD_HEAD_DIM
    gw = (SSD_HEADS // SSD_GROUPS) * hp
    b_off = SSD_WIDTH
    c_off = SSD_WIDTH + SSD_GROUPS * SSD_STATE

    @pl.when(pl.program_id(1) == 0)
    def _():
        xext[0:CONV_TAIL, :] = jnp.zeros((CONV_TAIL, SSD_CONV_CH), F32)
        state[...] = jnp.zeros_like(state)

    xext[CONV_TAIL:CONV_TAIL + L, :] = xbc_ref[...].astype(F32)
    cblk = 512
    for c0 in range(0, SSD_CONV_CH, cblk):
        cs = slice(c0, c0 + cblk)
        acc = cb_ref[:, cs] + cw_ref[0:1, cs] * xext[CONV_TAIL - 3:CONV_TAIL - 3 + L, cs]
        for k in range(1, SSD_CONV):
            r0 = CONV_TAIL - 3 + k
            acc = acc + cw_ref[k:k + 1, cs] * xext[r0:r0 + L, cs]
        xc[:, cs] = _silu(acc)
    xext[0:CONV_TAIL, :] = xext[L:L + CONV_TAIL, :]

    dt = _softplus(dt_ref[...] + dtb_ref[...])
    a = jnp.exp(alog_ref[...]) * (-LOG2E)
    a_cs = _split3_dot(tril_ref[...], dt * a)
    a_cs_t = a_cs.T
    a_end = a_cs[L - 1:L, :]
    exp_a = jnp.exp2(a_cs)
    dt_to_end = dt * jnp.exp2(a_end - a_cs)
    chunk_decay = jnp.exp2(a_end)

    row = lax.broadcasted_iota(jnp.int32, (L, L), 0)
    col = lax.broadcasted_iota(jnp.int32, (L, L), 1)
    causal = col <= row
    lane_lo = lax.broadcasted_iota(jnp.int32, (L, LANES), 1) < hp
    lane_lo_row = lax.broadcasted_iota(jnp.int32, (1, LANES), 1) < hp

    for g in range(SSD_GROUPS):
        b_g = xc[:, b_off + g * SSD_STATE:b_off + (g + 1) * SSD_STATE]
        c_g = xc[:, c_off + g * SSD_STATE:c_off + (g + 1) * SSD_STATE].astype(BF16)
        cb = _dot_nt(c_g, b_g.astype(BF16))
        b_g_t = b_g.T.astype(BF16)
        y_off = _dot(c_g, state[:, g * gw:(g + 1) * gw].astype(BF16))
        for k in range(gw // LANES):
            h0 = g * (SSD_HEADS // SSD_GROUPS) + 2 * k
            cs = slice(g * gw + k * LANES, g * gw + (k + 1) * LANES)
            x_pair = xc[:, cs]
            xdt = (x_pair * _pair_cols(dt, h0, lane_lo)).astype(BF16)
            ys = []
            for h in (h0, h0 + 1):
                seg = jnp.broadcast_to(a_cs[:, h:h + 1], (L, L)) - jnp.broadcast_to(a_cs_t[h:h + 1, :], (L, L))
                decay = jnp.exp2(jnp.where(causal, seg, NEG_BIG))
                ys.append(_dot((cb * decay).astype(BF16), xdt))
            y = jnp.where(lane_lo, ys[0], ys[1])
            y = y + y_off[:, k * LANES:(k + 1) * LANES] * _pair_cols(exp_a, h0, lane_lo)
            y_scr[:, cs] = y + dsk_ref[:, cs] * x_pair
            x_to_end = (x_pair * _pair_cols(dt_to_end, h0, lane_lo)).astype(BF16)
            cd = jnp.where(lane_lo_row,
                           jnp.broadcast_to(chunk_decay[:, h0:h0 + 1], (1, LANES)),
                           jnp.broadcast_to(chunk_decay[:, h0 + 1:h0 + 2], (1, LANES)))
            state[:, cs] = state[:, cs] * cd + _dot(b_g_t, x_to_end)

    zz = z_ref[...].astype(F32)
    o_ref[...] = _rms(y_scr[...] * _silu(zz), g_ref[...]).astype(o_ref.dtype)


def ssd_mixer(proj, small, conv_w, conv_b, dt_bias_pad, a_log_pad, d_skip_rep, norm_g, *, batch, seq,
              z_blk, xbc_blk, dt_blk):
    L = SSD_CHUNK
    nc = seq // L
    tril = jnp.asarray(np.tril(np.ones((L, L), np.float32)), BF16)
    const = lambda b, c: (0, 0)
    return pl.pallas_call(
        _ssd_kernel,
        grid=(batch, nc),
        in_specs=[pl.BlockSpec((L, SSD_WIDTH), lambda b, c: (b * nc + c, z_blk)),
                  pl.BlockSpec((L, SSD_CONV_CH), lambda b, c: (b * nc + c, xbc_blk)),
                  pl.BlockSpec((L, LANES), lambda b, c: (b * nc + c, dt_blk)),
                  pl.BlockSpec((SSD_CONV, SSD_CONV_CH), const),
                  pl.BlockSpec((1, SSD_CONV_CH), const),
                  pl.BlockSpec((1, LANES), const),
                  pl.BlockSpec((1, LANES), const),
                  pl.BlockSpec((1, SSD_WIDTH), const),
                  pl.BlockSpec((1, SSD_WIDTH), const),
                  pl.BlockSpec((L, L), const)],
        out_specs=pl.BlockSpec((L, SSD_WIDTH), lambda b, c: (b * nc + c, 0)),
        out_shape=jax.ShapeDtypeStruct((batch * seq, SSD_WIDTH), BF16),
        scratch_shapes=[pltpu.VMEM((CONV_TAIL + L, SSD_CONV_CH), F32),
                        pltpu.VMEM((L, SSD_CONV_CH), F32),
                        pltpu.VMEM((SSD_STATE, SSD_WIDTH), F32),
                        pltpu.VMEM((L, SSD_WIDTH), F32)],
        compiler_params=_cparams("parallel", "arbitrary"),
    )(proj, proj, small, conv_w, conv_b, dt_bias_pad, a_log_pad, d_skip_rep, norm_g, tril)


def _rope(x, c, sa, sb):
    half = MLA_ROPE // 2
    return x * c + pltpu.roll(x, LANES - half, 1) * sa + pltpu.roll(x, half, 1) * sb


def _mla_prep_kernel(cq_ref, ckv_ref, kr_ref, qg_ref, kvg_ref, wqt_ref, wkn_ref, wvt_ref, c_ref, sa_ref, sb_ref,
                     cos_t_ref, sin_t_ref, qt_ref, kn_ref, vt_ref, kp_ref):
    scale = float((MLA_NOPE + MLA_ROPE) ** -0.5) * LOG2E
    hq = _rms(cq_ref[...].astype(F32), qg_ref[...])
    hkv = _rms(ckv_ref[...].astype(F32), kvg_ref[...])
    hq_t = hq.T.astype(BF16)
    hkv_t = hkv.T.astype(BF16)
    hkv = hkv.astype(BF16)
    cos_t, sin_t = cos_t_ref[...], sin_t_ref[...]
    tm = cq_ref.shape[0]
    qd = MLA_NOPE + LANES
    half = MLA_ROPE // 2
    rblk = 512
    for r in range(0, MLA_HEADS * qd, rblk):
        q = _dot(wqt_ref[r:r + rblk, :], hq_t) * scale
        for h0 in range(0, rblk, qd):
            x1 = q[h0 + MLA_NOPE:h0 + MLA_NOPE + half, :]
            x2 = q[h0 + MLA_NOPE + half:h0 + MLA_NOPE + MLA_ROPE, :]
            o = r + h0
            qt_ref[o:o + MLA_NOPE, :] = q[h0:h0 + MLA_NOPE, :].astype(BF16)
            qt_ref[o + MLA_NOPE:o + MLA_NOPE + half, :] = (x1 * cos_t - x2 * sin_t).astype(BF16)
            qt_ref[o + MLA_NOPE + half:o + MLA_NOPE + MLA_ROPE, :] = (x1 * sin_t + x2 * cos_t).astype(BF16)
            qt_ref[o + MLA_NOPE + MLA_ROPE:o + qd, :] = jnp.zeros((qd - MLA_NOPE - MLA_ROPE, tm), BF16)
    for r in range(0, MLA_HEADS * MLA_V, rblk):
        vt_ref[r:r + rblk, :] = _dot(wvt_ref[r:r + rblk, :], hkv_t).astype(BF16)
    for n in range(0, MLA_HEADS * MLA_NOPE, rblk):
        kn_ref[:, n:n + rblk] = _dot(hkv, wkn_ref[:, n:n + rblk]).astype(BF16)
    kp_ref[...] = _rope(kr_ref[...], c_ref[...], sa_ref[...], sb_ref[...]).astype(BF16)


def mla_prep(proj, small, q_norm, kv_norm, wq_t, wkn, wv_t, rope_c, rope_sa, rope_sb, cos_t, sin_t, *, seq,
             cq_blk, ckv_blk, kr_blk, tm):
    t = proj.shape[0]
    npos = seq // tm
    const = lambda i: (0, 0)
    pos = lambda i: (i % npos, 0)
    pos_t = lambda i: (0, i % npos)
    half = MLA_ROPE // 2
    nt = t // tm
    qtw = MLA_HEADS * (MLA_NOPE + LANES)
    knw = MLA_HEADS * MLA_NOPE
    vw = MLA_HEADS * MLA_V
    return pl.pallas_call(
        _mla_prep_kernel,
        grid=(t // tm,),
        in_specs=[pl.BlockSpec((tm, MLA_RANK), lambda i: (i, cq_blk)),
                  pl.BlockSpec((tm, MLA_RANK), lambda i: (i, ckv_blk)),
                  pl.BlockSpec((tm, LANES), lambda i: (i, kr_blk)),
                  pl.BlockSpec((1, MLA_RANK), const),
                  pl.BlockSpec((1, MLA_RANK), const),
                  pl.BlockSpec(wq_t.shape, const),
                  pl.BlockSpec(wkn.shape, const),
                  pl.BlockSpec(wv_t.shape, const),
                  pl.BlockSpec((tm, LANES), pos),
                  pl.BlockSpec((tm, LANES), pos),
                  pl.BlockSpec((tm, LANES), pos),
                  pl.BlockSpec((half, tm), pos_t),
                  pl.BlockSpec((half, tm), pos_t)],
        out_specs=[pl.BlockSpec((None, qtw, tm), lambda i: (i, 0, 0)),
                   pl.BlockSpec((tm, knw), lambda i: (i, 0)),
                   pl.BlockSpec((None, vw, tm), lambda i: (i, 0, 0)),
                   pl.BlockSpec((tm, LANES), lambda i: (i, 0))],
        out_shape=[jax.ShapeDtypeStruct((nt, qtw, tm), BF16),
                   jax.ShapeDtypeStruct((t, knw), BF16),
                   jax.ShapeDtypeStruct((nt, vw, tm), BF16),
                   jax.ShapeDtypeStruct((t, LANES), BF16)],
        compiler_params=_cparams("parallel"),
    )(proj, proj, small, q_norm, kv_norm, wq_t, wkn, wv_t, rope_c, rope_sa, rope_sb, cos_t, sin_t)


MLA_HEADS_PER_STEP = 4
MLA_TILE = 512
MLA_ONES_ROWS = 16


def _mla_attn_kernel(qt_ref, kn_ref, kp_ref, vt_ref, o_ref, *, tq):
    i = pl.program_id(2)
    heads = range(MLA_HEADS_PER_STEP)
    qd = MLA_NOPE + LANES
    ones = jnp.ones((MLA_ONES_ROWS, tq), BF16)

    def block(kb, carry, diagonal):
        ks = pl.multiple_of(kb * tq, tq)
        kp = kp_ref[pl.ds(ks, tq), :]
        sts = [_dot(jnp.concatenate([kn_ref[pl.ds(ks, tq), h * MLA_NOPE:(h + 1) * MLA_NOPE], kp], axis=1),
                    qt_ref[h * qd:(h + 1) * qd, :]) for h in heads]
        if diagonal:
            key = lax.broadcasted_iota(jnp.int32, (tq, tq), 0)
            qry = lax.broadcasted_iota(jnp.int32, (tq, tq), 1)
            sts = [jnp.where(key <= qry, st, NEG_BIG) for st in sts]
        stats = []
        for h in heads:
            m = carry[h][0]
            m_new = jnp.maximum(m, jnp.max(sts[h], axis=0, keepdims=True))
            stats.append((m_new, jnp.exp2(m - m_new), jnp.exp2(sts[h] - m_new).astype(BF16)))
        out = []
        for h in heads:
            m_new, alpha, pt = stats[h]
            vt1 = jnp.concatenate([vt_ref[kb, h * MLA_V:(h + 1) * MLA_V, :], ones], axis=0)
            out.append((m_new, carry[h][1] * alpha + _dot(vt1, pt)))
        return tuple(out)

    init = tuple((jnp.full((1, tq), NEG_BIG, F32), jnp.zeros((MLA_V + MLA_ONES_ROWS, tq), F32)) for _ in heads)
    carry = lax.fori_loop(0, i, lambda kb, c: block(kb, c, False), init)
    carry = block(i, carry, True)
    for h in heads:
        acc = carry[h][1]
        out_t = acc[:MLA_V, :] / acc[MLA_V:MLA_V + 1, :]
        o_ref[:, h * MLA_V:(h + 1) * MLA_V] = out_t.T.astype(o_ref.dtype)


def mla_attention(qt, kn, kp, vt, *, batch, seq, tq):
    nq = seq // tq
    t = batch * seq
    nh = MLA_HEADS_PER_STEP
    qd = MLA_NOPE + LANES
    return pl.pallas_call(
        functools.partial(_mla_attn_kernel, tq=tq),
        grid=(batch, MLA_HEADS // nh, nq),
        in_specs=[pl.BlockSpec((None, nh * qd, tq), lambda b, h, i: (b * nq + i, h, 0)),
                  pl.BlockSpec((seq, nh * MLA_NOPE), lambda b, h, i: (b, h)),
                  pl.BlockSpec((seq, LANES), lambda b, h, i: (b, 0)),
                  pl.BlockSpec((nq, nh * MLA_V, tq), lambda b, h, i: (b, h, 0))],
        out_specs=pl.BlockSpec((tq, nh * MLA_V), lambda b, h, i: (b * nq + i, h)),
        out_shape=jax.ShapeDtypeStruct((t, MLA_HEADS * MLA_V), BF16),
        compiler_params=_cparams("parallel", "parallel", "arbitrary"),
    )(qt, kn, kp, vt)


SB_HEADS_PER_STEP = 4


def _sb_attn_kernel(q_ref, k_ref, v_ref, u_ref, o_ref, *, tq):
    i = pl.program_id(2)
    u = u_ref[...]
    hd = SB_HEAD_DIM
    heads = range(SB_HEADS_PER_STEP)

    def block(kb, carries, accs, masked):
        ks = pl.multiple_of(kb * tq, tq)
        if masked:
            row = lax.broadcasted_iota(jnp.int32, (tq, tq), 0)
            col = lax.broadcasted_iota(jnp.int32, (tq, tq), 1)
            keep = col < row
        cols = [slice(h * hd, (h + 1) * hd) for h in heads]
        zs = [_dot_nt(q_ref[:, c], k_ref[pl.ds(ks, tq), c]) for c in cols]
        sps = [_softplus_log2(z) for z in zs]
        if masked:
            sps = [jnp.where(keep, sp, 0.0) for sp in sps]
        his = [sp.astype(BF16) for sp in sps]
        los = [(sp - hi.astype(F32)).astype(BF16) for sp, hi in zip(sps, his)]
        rsums = [_dot(hi, u) + _dot(lo, u) for hi, lo in zip(his, los)]
        ws = []
        for h in heads:
            carry_b = jnp.concatenate([carries[h]] * (tq // LANES), axis=1)
            w = jnp.exp2(zs[h] - rsums[h] - carry_b)
            ws.append(jnp.where(keep, w, 0.0) if masked else w)
        new_accs = [accs[h] + _dot(ws[h].astype(BF16), v_ref[pl.ds(ks, tq), cols[h]]) for h in heads]
        new_carries = [carries[h] + jnp.broadcast_to(rsums[h][:, 0:1], (tq, LANES)) for h in heads]
        return tuple(new_carries), tuple(new_accs)

    def live(carries):
        return jnp.min(functools.reduce(jnp.minimum, carries))

    zeros = tuple(jnp.zeros((tq, LANES), F32) for _ in heads)
    carries, accs = block(i, zeros, zeros, True)

    def cond(st):
        kb, least, _, _ = st
        return jnp.logical_and(kb >= 0, least < SB_DEAD_LOG2)

    def body(st):
        kb, _, carries, accs = st
        carries, accs = block(kb, carries, accs, False)
        return kb - 1, live(carries), carries, accs

    _, _, _, accs = lax.while_loop(cond, body, (i - 1, live(carries), carries, accs))
    for h in heads:
        o_ref[:, h * hd:(h + 1) * hd] = accs[h].astype(o_ref.dtype)


def sb_attention(qkv, *, batch, seq, tq=256):
    tq = min(tq, seq)
    nq = seq // tq
    t = batch * seq
    w = SB_HEADS_PER_STEP * SB_HEAD_DIM
    nhb = SB_HEADS // SB_HEADS_PER_STEP
    u = jnp.asarray(np.tril(np.ones((tq, tq), np.float32)), BF16)
    return pl.pallas_call(
        functools.partial(_sb_attn_kernel, tq=tq),
        grid=(batch, nhb, nq),
        in_specs=[pl.BlockSpec((tq, w), lambda b, h, i: (b * nq + i, h)),
                  pl.BlockSpec((seq, w), lambda b, h, i: (b, nhb + h)),
                  pl.BlockSpec((seq, w), lambda b, h, i: (b, 2 * nhb + h)),
                  pl.BlockSpec((tq, tq), lambda b, h, i: (0, 0))],
        out_specs=pl.BlockSpec((tq, w), lambda b, h, i: (b * nq + i, h)),
        out_shape=jax.ShapeDtypeStruct((t, SB_HEADS * SB_HEAD_DIM), BF16),
        compiler_params=_cparams("parallel", "parallel", "arbitrary"),
    )(qkv, qkv, qkv, u)


def _rope_tables(seq):
    half = MLA_ROPE // 2
    inv_freq = ROPE_THETA ** (-jnp.arange(half, dtype=F32) / half)
    ang = jnp.arange(seq, dtype=F32)[:, None] * inv_freq[None, :]
    cos, sin = jnp.cos(ang), jnp.sin(ang)
    zero = jnp.zeros_like(cos)
    pad = jnp.zeros((seq, LANES - MLA_ROPE), F32)
    c = jnp.concatenate([cos, cos, pad], axis=1)
    sa = jnp.concatenate([-sin, zero, pad], axis=1)
    sb = jnp.concatenate([zero, sin, pad], axis=1)
    return c, sa, sb, cos.T, sin.T


def _pad_lanes(v, width=LANES):
    v = v.reshape(1, -1).astype(F32)
    return jnp.pad(v, ((0, 0), (0, width - v.shape[1])))


def _even_layer(x, mix_g, w_in, conv_w, conv_b, dt_bias, a_log, d_skip, ssd_norm, q_norm, kv_norm,
                w_uq, w_ukv, w_out, rope, *, batch, seq):
    d = x.shape[1]
    o_z, o_xbc, o_dt, o_cq, o_ckv, o_kr = np.cumsum(
        [0, SSD_WIDTH, SSD_CONV_CH, SSD_HEADS, MLA_RANK, MLA_RANK]).tolist()
    w_main = jnp.concatenate([w_in[:, o_z:o_xbc], w_in[:, o_cq:o_ckv], w_in[:, o_ckv:o_kr],
                              w_in[:, o_xbc:o_dt]], axis=1).astype(BF16)
    zeros = lambda n: jnp.zeros((d, n), F32)
    w_small = jnp.concatenate([w_in[:, o_kr:o_kr + MLA_ROPE], zeros(LANES - MLA_ROPE),
                               w_in[:, o_dt:o_cq], zeros(LANES - SSD_HEADS)], axis=1).astype(BF16)
    proj, small = norm_matmul(x, mix_g, w_main, w_small)

    y_ssd = ssd_mixer(proj, small, conv_w, conv_b.reshape(1, -1), _pad_lanes(dt_bias), _pad_lanes(a_log),
                      jnp.repeat(d_skip, SSD_HEAD_DIM).reshape(1, -1), ssd_norm.reshape(1, -1),
                      batch=batch, seq=seq, z_blk=0, xbc_blk=1, dt_blk=1)

    wq3 = w_uq.reshape(MLA_RANK, MLA_HEADS, MLA_NOPE + MLA_ROPE)
    wq_t = jnp.pad(wq3, ((0, 0), (0, 0), (0, LANES - MLA_ROPE))).reshape(MLA_RANK, -1).T.astype(BF16)
    wkv3 = w_ukv.reshape(MLA_RANK, MLA_HEADS, MLA_NOPE + MLA_V)
    wkn = wkv3[:, :, :MLA_NOPE].reshape(MLA_RANK, -1).astype(BF16)
    wv_t = wkv3[:, :, MLA_NOPE:].reshape(MLA_RANK, -1).T.astype(BF16)
    tq = min(MLA_TILE, seq)
    qt, kn, vt, kp = mla_prep(proj, small, q_norm.reshape(1, -1), kv_norm.reshape(1, -1), wq_t, wkn, wv_t,
                              *rope, seq=seq, cq_blk=SSD_WIDTH // MLA_RANK,
                              ckv_blk=SSD_WIDTH // MLA_RANK + 1, kr_blk=0, tm=tq)
    y_mla = mla_attention(qt, kn, kp, vt, batch=batch, seq=seq, tq=tq)

    w_out = w_out.astype(BF16)
    return matmul_residual([y_ssd, y_mla], [w_out[:SSD_WIDTH], w_out[SSD_WIDTH:]], x)


def _odd_layer(x, mix_g, w_qkv, w_out, *, batch, seq):
    width = SB_HEADS * SB_HEAD_DIM
    scale = SB_HEAD_DIM ** -0.5 * LOG2E
    w = jnp.concatenate([w_qkv[:, :width] * scale, w_qkv[:, width:]], axis=1).astype(BF16)
    qkv = norm_matmul(x, mix_g, w)
    y = sb_attention(qkv, batch=batch, seq=seq)
    return matmul_residual([y], [w_out.astype(BF16)], x)


def _ffn(x, g, w_gate, w_up, w_down, layer, final_g=None):
    wg, wu, wd = (layer_weight_bf16(w, layer) for w in (w_gate, w_up, w_down))
    return ffn(x, g, wg, wu, wd, final_g)


def kernel(x, mix_norm, ffn_norm, w_in, conv_w, conv_b, dt_bias, a_log, d_skip, ssd_norm, q_norm, kv_norm,
           w_uq, w_ukv, w_out_even, w_qkv, w_out_odd, w_gate, w_up, w_down, final_norm):
    batch, seq, d = x.shape
    depth = mix_norm.shape[0]
    rope = _rope_tables(seq)
    h = x.reshape(batch * seq, d)
    for layer in range(depth):
        mix_g = mix_norm[layer].reshape(1, -1)
        if layer % 2 == 0:
            e = layer // 2
            h = _even_layer(h, mix_g, w_in[e], conv_w[e], conv_b[e], dt_bias[e], a_log[e], d_skip[e],
                            ssd_norm[e], q_norm[e], kv_norm[e], w_uq[e], w_ukv[e], w_out_even[e], rope,
                            batch=batch, seq=seq)
        else:
            o = layer // 2
            h = _odd_layer(h, mix_g, w_qkv[o], w_out_odd[o], batch=batch, seq=seq)
        final_g = final_norm.reshape(1, -1) if layer == depth - 1 else None
        h = _ffn(h, ffn_norm[layer].reshape(1, -1), w_gate, w_up, w_down, layer, final_g)
    return h.reshape(batch, seq, d)
```

```python
import functools

import jax
import jax.numpy as jnp
import numpy as np
from jax import lax
from jax.experimental import pallas as pl
from jax.experimental.pallas import tpu as pltpu

F32 = jnp.float32
BF16 = jnp.bfloat16

RMS_EPS = 1e-6
ROPE_THETA = 10000.0

SSD_HEADS = 32
SSD_HEAD_DIM = 64
SSD_WIDTH = SSD_HEADS * SSD_HEAD_DIM
SSD_GROUPS = 4
SSD_STATE = 128
SSD_CONV = 4
SSD_CHUNK = 128
SSD_CONV_CH = SSD_WIDTH + 2 * SSD_GROUPS * SSD_STATE

MLA_HEADS = 16
MLA_RANK = 512
MLA_NOPE = 128
MLA_ROPE = 64
MLA_V = 128

SB_HEADS = 16
SB_HEAD_DIM = 128

LANES = 128
CONV_TAIL = 8
VMEM_LIMIT = 56 * 1024 * 1024
NCHUNK = 512
LOG2E = 1.4426950408889634
SB_DEAD_LOG2 = 151.0
NEG_BIG = -1e30

NT_DIMS = (((1,), (1,)), ((), ()))


def _cparams(*sem):
    return pltpu.CompilerParams(dimension_semantics=sem, vmem_limit_bytes=VMEM_LIMIT)


def _dot(a, b):
    return jnp.dot(a, b, preferred_element_type=F32)


def _dot_nt(a, b):
    return lax.dot_general(a, b, NT_DIMS, preferred_element_type=F32)


def _rms(x, g):
    ms = jnp.mean(x * x, axis=-1, keepdims=True)
    return x * lax.rsqrt(ms + RMS_EPS) * g


def _silu(x):
    return x / (1.0 + jnp.exp2(x * (-LOG2E)))


def _softplus(x):
    return jnp.maximum(x, 0.0) + jnp.log1p(jnp.exp(-jnp.abs(x)))


def _softplus_log2(z2):
    return jnp.maximum(z2, 0.0) + jnp.log(1.0 + jnp.exp2(-jnp.abs(z2))) * LOG2E


def _split3_dot(tri, x):
    x1 = x.astype(BF16)
    r1 = x - x1.astype(F32)
    x2 = r1.astype(BF16)
    x3 = (r1 - x2.astype(F32)).astype(BF16)
    return _dot(tri, x1) + _dot(tri, x2) + _dot(tri, x3)


def _norm_rows_into(x_ref, g_ref, h_scr, rows):
    tm = x_ref.shape[0]
    for r in range(0, tm, rows):
        x = x_ref[r:r + rows, :].astype(F32)
        h_scr[r:r + rows, :] = _rms(x, g_ref[...]).astype(BF16)


def _dot_cols_into(o_ref, a, w_ref, accumulate=False):
    n = w_ref.shape[1]
    step = min(NCHUNK, n)
    for c in range(0, n, step):
        r = _dot(a, w_ref[:, c:c + step])
        if accumulate:
            o_ref[:, c:c + step] += r
        else:
            o_ref[:, c:c + step] = r.astype(o_ref.dtype)


def _norm_mm_kernel(x_ref, g_ref, w_ref, o_ref, h_scr, *, rows):
    @pl.when(pl.program_id(1) == 0)
    def _():
        _norm_rows_into(x_ref, g_ref, h_scr, rows)

    _dot_cols_into(o_ref, h_scr[...], w_ref)


def _norm_mm2_kernel(x_ref, g_ref, w_ref, ws_ref, o_ref, os_ref, h_scr, *, rows):
    @pl.when(pl.program_id(1) == 0)
    def _():
        _norm_rows_into(x_ref, g_ref, h_scr, rows)
        os_ref[...] = _dot(h_scr[...], ws_ref[...])

    _dot_cols_into(o_ref, h_scr[...], w_ref)


def norm_matmul(x, g, w, w_small=None, *, tm=1024, tn=2048):
    t, k = x.shape
    n = w.shape[1]
    tm = min(tm, t)
    rows = min(256, tm)
    grid = (t // tm, n // tn)
    x_spec = pl.BlockSpec((tm, k), lambda i, j: (i, 0))
    g_spec = pl.BlockSpec((1, k), lambda i, j: (0, 0))
    w_spec = pl.BlockSpec((k, tn), lambda i, j: (0, j))
    o_spec = pl.BlockSpec((tm, tn), lambda i, j: (i, j))
    scratch = [pltpu.VMEM((tm, k), BF16)]
    if w_small is None:
        return pl.pallas_call(
            functools.partial(_norm_mm_kernel, rows=rows),
            grid=grid, in_specs=[x_spec, g_spec, w_spec], out_specs=o_spec,
            out_shape=jax.ShapeDtypeStruct((t, n), BF16), scratch_shapes=scratch,
            compiler_params=_cparams("parallel", "arbitrary"),
        )(x, g, w)
    ns = w_small.shape[1]
    return pl.pallas_call(
        functools.partial(_norm_mm2_kernel, rows=rows),
        grid=grid,
        in_specs=[x_spec, g_spec, w_spec, pl.BlockSpec((k, ns), lambda i, j: (0, 0))],
        out_specs=[o_spec, pl.BlockSpec((tm, ns), lambda i, j: (i, 0))],
        out_shape=[jax.ShapeDtypeStruct((t, n), BF16), jax.ShapeDtypeStruct((t, ns), F32)],
        scratch_shapes=scratch,
        compiler_params=_cparams("parallel", "arbitrary"),
    )(x, g, w, w_small)


def _cast_kernel(w_ref, o_ref):
    o_ref[...] = w_ref[...].astype(o_ref.dtype)


def layer_weight_bf16(w_stack, layer, *, row0=0, rows=None, tr=256):
    _, k, n = w_stack.shape
    k = k - row0 if rows is None else rows
    tr = min(tr, k)
    blk0 = row0 // tr
    return pl.pallas_call(
        _cast_kernel,
        grid=(k // tr,),
        in_specs=[pl.BlockSpec((None, tr, n), lambda i: (layer, blk0 + i, 0))],
        out_specs=pl.BlockSpec((tr, n), lambda i: (i, 0)),
        out_shape=jax.ShapeDtypeStruct((k, n), BF16),
        compiler_params=_cparams("parallel"),
    )(w_stack)


def _ffn_kernel(*refs, rows, final):
    if final:
        x_ref, g_ref, wg_ref, wu_ref, wd_ref, fg_ref, o_ref, h_scr = refs
    else:
        x_ref, g_ref, wg_ref, wu_ref, wd_ref, o_ref, h_scr = refs
    c = pl.program_id(1)

    @pl.when(c == 0)
    def _():
        _norm_rows_into(x_ref, g_ref, h_scr, rows)
        o_ref[...] = x_ref[...]

    h = h_scr[...]
    hid = (_silu(_dot(h, wg_ref[...])) * _dot(h, wu_ref[...])).astype(BF16)
    _dot_cols_into(o_ref, hid, wd_ref, accumulate=True)

    if final:
        @pl.when(c == pl.num_programs(1) - 1)
        def _():
            for r in range(0, o_ref.shape[0], rows):
                o_ref[r:r + rows, :] = _rms(o_ref[r:r + rows, :], fg_ref[...])


def ffn(x, g, wg, wu, wd, final_g=None, *, tm=1024, th=512):
    t, k = x.shape
    hidden = wg.shape[1]
    tm = min(tm, t)
    rows = min(256, tm)
    final = final_g is not None
    const = lambda i, c: (0, 0)
    in_specs = [pl.BlockSpec((tm, k), lambda i, c: (i, 0), pipeline_mode=pl.Buffered(1)),
                pl.BlockSpec((1, k), const),
                pl.BlockSpec((k, th), lambda i, c: (0, c)),
                pl.BlockSpec((k, th), lambda i, c: (0, c)),
                pl.BlockSpec((th, k), lambda i, c: (c, 0))]
    args = [x, g, wg, wu, wd]
    if final:
        in_specs.append(pl.BlockSpec((1, k), const))
        args.append(final_g)
    return pl.pallas_call(
        functools.partial(_ffn_kernel, rows=rows, final=final),
        grid=(t // tm, hidden // th),
        in_specs=in_specs,
        out_specs=pl.BlockSpec((tm, k), lambda i, c: (i, 0)),
        out_shape=jax.ShapeDtypeStruct((t, k), F32),
        scratch_shapes=[pltpu.VMEM((tm, k), BF16)],
        compiler_params=_cparams("parallel", "arbitrary"),
    )(*args)


def _mm_res_kernel(*refs, n_pairs):
    a_refs = refs[:n_pairs]
    w_refs = refs[n_pairs:2 * n_pairs]
    r_ref, o_ref = refs[2 * n_pairs], refs[2 * n_pairs + 1]
    n = o_ref.shape[1]
    for c in range(0, n, NCHUNK):
        acc = r_ref[:, c:c + NCHUNK]
        for a_ref, w_ref in zip(a_refs, w_refs):
            acc = acc + _dot(a_ref[...], w_ref[:, c:c + NCHUNK])
        o_ref[:, c:c + NCHUNK] = acc


def matmul_residual(a_list, w_list, res, *, tm=512):
    t, n = res.shape
    tm = min(tm, t)
    n_pairs = len(a_list)
    in_specs = [pl.BlockSpec((tm, a.shape[1]), lambda i: (i, 0)) for a in a_list]
    in_specs += [pl.BlockSpec(w.shape, lambda i: (0, 0)) for w in w_list]
    in_specs += [pl.BlockSpec((tm, n), lambda i: (i, 0))]
    return pl.pallas_call(
        functools.partial(_mm_res_kernel, n_pairs=n_pairs),
        grid=(t // tm,),
        in_specs=in_specs,
        out_specs=pl.BlockSpec((tm, n), lambda i: (i, 0)),
        out_shape=jax.ShapeDtypeStruct((t, n), F32),
        compiler_params=_cparams("parallel"),
    )(*a_list, *w_list, res)


def _pair_cols(arr, h0, lane_lo):
    rows = arr.shape[0]
    c0 = jnp.broadcast_to(arr[:, h0:h0 + 1], (rows, LANES))
    c1 = jnp.broadcast_to(arr[:, h0 + 1:h0 + 2], (rows, LANES))
    return jnp.where(lane_lo, c0, c1)


def _ssd_kernel(z_ref, xbc_ref, dt_ref, cw_ref, cb_ref, dtb_ref, alog_ref, dsk_ref, g_ref, tril_ref,
                o_ref, xext, xc, state, y_scr):
    L = SSD_CHUNK
    hp = SSD_HEAD_DIM
    gw = (SSD_HEADS // SSD_GROUPS) * hp
    b_off = SSD_WIDTH
    c_off = SSD_WIDTH + SSD_GROUPS * SSD_STATE

    @pl.when(pl.program_id(1) == 0)
    def _():
        xext[0:CONV_TAIL, :] = jnp.zeros((CONV_TAIL, SSD_CONV_CH), F32)
        state[...] = jnp.zeros_like(state)

    xext[CONV_TAIL:CONV_TAIL + L, :] = xbc_ref[...].astype(F32)
    cblk = 512
    for c0 in range(0, SSD_CONV_CH, cblk):
        cs = slice(c0, c0 + cblk)
        acc = cb_ref[:, cs] + cw_ref[0:1, cs] * xext[CONV_TAIL - 3:CONV_TAIL - 3 + L, cs]
        for k in range(1, SSD_CONV):
            r0 = CONV_TAIL - 3 + k
            acc = acc + cw_ref[k:k + 1, cs] * xext[r0:r0 + L, cs]
        xc[:, cs] = _silu(acc)
    xext[0:CONV_TAIL, :] = xext[L:L + CONV_TAIL, :]

    dt = _softplus(dt_ref[...] + dtb_ref[...])
    a = jnp.exp(alog_ref[...]) * (-LOG2E)
    a_cs = _split3_dot(tril_ref[...], dt * a)
    a_cs_t = a_cs.T
    a_end = a_cs[L - 1:L, :]
    exp_a = jnp.exp2(a_cs)
    dt_to_end = dt * jnp.exp2(a_end - a_cs)
    chunk_decay = jnp.exp2(a_end)

    row = lax.broadcasted_iota(jnp.int32, (L, L), 0)
    col = lax.broadcasted_iota(jnp.int32, (L, L), 1)
    causal = col <= row
    lane_lo = lax.broadcasted_iota(jnp.int32, (L, LANES), 1) < hp
    lane_lo_row = lax.broadcasted_iota(jnp.int32, (1, LANES), 1) < hp

    for g in range(SSD_GROUPS):
        b_g = xc[:, b_off + g * SSD_STATE:b_off + (g + 1) * SSD_STATE]
        c_g = xc[:, c_off + g * SSD_STATE:c_off + (g + 1) * SSD_STATE].astype(BF16)
        cb = _dot_nt(c_g, b_g.astype(BF16))
        b_g_t = b_g.T.astype(BF16)
        y_off = _dot(c_g, state[:, g * gw:(g + 1) * gw].astype(BF16))
        for k in range(gw // LANES):
            h0 = g * (SSD_HEADS // SSD_GROUPS) + 2 * k
            cs = slice(g * gw + k * LANES, g * gw + (k + 1) * LANES)
            x_pair = xc[:, cs]
            xdt = (x_pair * _pair_cols(dt, h0, lane_lo)).astype(BF16)
            ys = []
            for h in (h0, h0 + 1):
                seg = jnp.broadcast_to(a_cs[:, h:h + 1], (L, L)) - jnp.broadcast_to(a_cs_t[h:h + 1, :], (L, L))
                decay = jnp.exp2(jnp.where(causal, seg, NEG_BIG))
                ys.append(_dot((cb * decay).astype(BF16), xdt))
            y = jnp.where(lane_lo, ys[0], ys[1])
            y = y + y_off[:, k * LANES:(k + 1) * LANES] * _pair_cols(exp_a, h0, lane_lo)
            y_scr[:, cs] = y + dsk_ref[:, cs] * x_pair
            x_to_end = (x_pair * _pair_cols(dt_to_end, h0, lane_lo)).astype(BF16)
            cd = jnp.where(lane_lo_row,
                           jnp.broadcast_to(chunk_decay[:, h0:h0 + 1], (1, LANES)),
                           jnp.broadcast_to(chunk_decay[:, h0 + 1:h0 + 2], (1, LANES)))
            state[:, cs] = state[:, cs] * cd + _dot(b_g_t, x_to_end)

    zz = z_ref[...].astype(F32)
    o_ref[...] = _rms(y_scr[...] * _silu(zz), g_ref[...]).astype(o_ref.dtype)


def ssd_mixer(proj, small, conv_w, conv_b, dt_bias_pad, a_log_pad, d_skip_rep, norm_g, *, batch, seq,
              z_blk, xbc_blk, dt_blk):
    L = SSD_CHUNK
    nc = seq // L
    tril = jnp.asarray(np.tril(np.ones((L, L), np.float32)), BF16)
    const = lambda b, c: (0, 0)
    return pl.pallas_call(
        _ssd_kernel,
        grid=(batch, nc),
        in_specs=[pl.BlockSpec((L, SSD_WIDTH), lambda b, c: (b * nc + c, z_blk)),
                  pl.BlockSpec((L, SSD_CONV_CH), lambda b, c: (b * nc + c, xbc_blk)),
                  pl.BlockSpec((L, LANES), lambda b, c: (b * nc + c, dt_blk)),
                  pl.BlockSpec((SSD_CONV, SSD_CONV_CH), const),
                  pl.BlockSpec((1, SSD_CONV_CH), const),
                  pl.BlockSpec((1, LANES), const),
                  pl.BlockSpec((1, LANES), const),
                  pl.BlockSpec((1, SSD_WIDTH), const),
                  pl.BlockSpec((1, SSD_WIDTH), const),
                  pl.BlockSpec((L, L), const)],
        out_specs=pl.BlockSpec((L, SSD_WIDTH), lambda b, c: (b * nc + c, 0)),
        out_shape=jax.ShapeDtypeStruct((batch * seq, SSD_WIDTH), BF16),
        scratch_shapes=[pltpu.VMEM((CONV_TAIL + L, SSD_CONV_CH), F32),
                        pltpu.VMEM((L, SSD_CONV_CH), F32),
                        pltpu.VMEM((SSD_STATE, SSD_WIDTH), F32),
                        pltpu.VMEM((L, SSD_WIDTH), F32)],
        compiler_params=_cparams("parallel", "arbitrary"),
    )(proj, proj, small, conv_w, conv_b, dt_bias_pad, a_log_pad, d_skip_rep, norm_g, tril)


def _rope(x, c, sa, sb):
    half = MLA_ROPE // 2
    return x * c + pltpu.roll(x, LANES - half, 1) * sa + pltpu.roll(x, half, 1) * sb


def _mla_prep_kernel(cq_ref, ckv_ref, kr_ref, qg_ref, kvg_ref, wqt_ref, wkn_ref, wvt_ref, c_ref, sa_ref, sb_ref,
                     cos_t_ref, sin_t_ref, qt_ref, kn_ref, vt_ref, kp_ref):
    scale = float((MLA_NOPE + MLA_ROPE) ** -0.5) * LOG2E
    hq = _rms(cq_ref[...].astype(F32), qg_ref[...])
    hkv = _rms(ckv_ref[...].astype(F32), kvg_ref[...])
    hq_t = hq.T.astype(BF16)
    hkv_t = hkv.T.astype(BF16)
    hkv = hkv.astype(BF16)
    cos_t, sin_t = cos_t_ref[...], sin_t_ref[...]
    tm = cq_ref.shape[0]
    qd = MLA_NOPE + LANES
    half = MLA_ROPE // 2
    rblk = 512
    for r in range(0, MLA_HEADS * qd, rblk):
        q = _dot(wqt_ref[r:r + rblk, :], hq_t) * scale
        for h0 in range(0, rblk, qd):
            x1 = q[h0 + MLA_NOPE:h0 + MLA_NOPE + half, :]
            x2 = q[h0 + MLA_NOPE + half:h0 + MLA_NOPE + MLA_ROPE, :]
            o = r + h0
            qt_ref[o:o + MLA_NOPE, :] = q[h0:h0 + MLA_NOPE, :].astype(BF16)
            qt_ref[o + MLA_NOPE:o + MLA_NOPE + half, :] = (x1 * cos_t - x2 * sin_t).astype(BF16)
            qt_ref[o + MLA_NOPE + half:o + MLA_NOPE + MLA_ROPE, :] = (x1 * sin_t + x2 * cos_t).astype(BF16)
            qt_ref[o + MLA_NOPE + MLA_ROPE:o + qd, :] = jnp.zeros((qd - MLA_NOPE - MLA_ROPE, tm), BF16)
    for r in range(0, MLA_HEADS * MLA_V, rblk):
        vt_ref[r:r + rblk, :] = _dot(wvt_ref[r:r + rblk, :], hkv_t).astype(BF16)
    for n in range(0, MLA_HEADS * MLA_NOPE, rblk):
        kn_ref[:, n:n + rblk] = _dot(hkv, wkn_ref[:, n:n + rblk]).astype(BF16)
    kp_ref[...] = _rope(kr_ref[...], c_ref[...], sa_ref[...], sb_ref[...]).astype(BF16)


def mla_prep(proj, small, q_norm, kv_norm, wq_t, wkn, wv_t, rope_c, rope_sa, rope_sb, cos_t, sin_t, *, seq,
             cq_blk, ckv_blk, kr_blk, tm):
    t = proj.shape[0]
    npos = seq // tm
    const = lambda i: (0, 0)
    pos = lambda i: (i % npos, 0)
    pos_t = lambda i: (0, i % npos)
    half = MLA_ROPE // 2
    nt = t // tm
    qtw = MLA_HEADS * (MLA_NOPE + LANES)
    knw = MLA_HEADS * MLA_NOPE
    vw = MLA_HEADS * MLA_V
    return pl.pallas_call(
        _mla_prep_kernel,
        grid=(t // tm,),
        in_specs=[pl.BlockSpec((tm, MLA_RANK), lambda i: (i, cq_blk)),
                  pl.BlockSpec((tm, MLA_RANK), lambda i: (i, ckv_blk)),
                  pl.BlockSpec((tm, LANES), lambda i: (i, kr_blk)),
                  pl.BlockSpec((1, MLA_RANK), const),
                  pl.BlockSpec((1, MLA_RANK), const),
                  pl.BlockSpec(wq_t.shape, const),
                  pl.BlockSpec(wkn.shape, const),
                  pl.BlockSpec(wv_t.shape, const),
                  pl.BlockSpec((tm, LANES), pos),
                  pl.BlockSpec((tm, LANES), pos),
                  pl.BlockSpec((tm, LANES), pos),
                  pl.BlockSpec((half, tm), pos_t),
                  pl.BlockSpec((half, tm), pos_t)],
        out_specs=[pl.BlockSpec((None, qtw, tm), lambda i: (i, 0, 0)),
                   pl.BlockSpec((tm, knw), lambda i: (i, 0)),
                   pl.BlockSpec((None, vw, tm), lambda i: (i, 0, 0)),
                   pl.BlockSpec((tm, LANES), lambda i: (i, 0))],
        out_shape=[jax.ShapeDtypeStruct((nt, qtw, tm), BF16),
                   jax.ShapeDtypeStruct((t, knw), BF16),
                   jax.ShapeDtypeStruct((nt, vw, tm), BF16),
                   jax.ShapeDtypeStruct((t, LANES), BF16)],
        compiler_params=_cparams("parallel"),
    )(proj, proj, small, q_norm, kv_norm, wq_t, wkn, wv_t, rope_c, rope_sa, rope_sb, cos_t, sin_t)


MLA_HEADS_PER_STEP = 4
MLA_TILE = 512
MLA_ONES_ROWS = 16


def _mla_attn_kernel(qt_ref, kn_ref, kp_ref, vt_ref, o_ref, *, tq):
    i = pl.program_id(2)
    heads = range(MLA_HEADS_PER_STEP)
    qd = MLA_NOPE + LANES
    ones = jnp.ones((MLA_ONES_ROWS, tq), BF16)

    def block(kb, carry, diagonal):
        ks = pl.multiple_of(kb * tq, tq)
        kp = kp_ref[pl.ds(ks, tq), :]
        sts = [_dot(jnp.concatenate([kn_ref[pl.ds(ks, tq), h * MLA_NOPE:(h + 1) * MLA_NOPE], kp], axis=1),
                    qt_ref[h * qd:(h + 1) * qd, :]) for h in heads]
        if diagonal:
            key = lax.broadcasted_iota(jnp.int32, (tq, tq), 0)
            qry = lax.broadcasted_iota(jnp.int32, (tq, tq), 1)
            sts = [jnp.where(key <= qry, st, NEG_BIG) for st in sts]
        stats = []
        for h in heads:
            m = carry[h][0]
            m_new = jnp.maximum(m, jnp.max(sts[h], axis=0, keepdims=True))
            stats.append((m_new, jnp.exp2(m - m_new), jnp.exp2(sts[h] - m_new).astype(BF16)))
        out = []
        for h in heads:
            m_new, alpha, pt = stats[h]
            vt1 = jnp.concatenate([vt_ref[kb, h * MLA_V:(h + 1) * MLA_V, :], ones], axis=0)
            out.append((m_new, carry[h][1] * alpha + _dot(vt1, pt)))
        return tuple(out)

    init = tuple((jnp.full((1, tq), NEG_BIG, F32), jnp.zeros((MLA_V + MLA_ONES_ROWS, tq), F32)) for _ in heads)
    carry = lax.fori_loop(0, i, lambda kb, c: block(kb, c, False), init)
    carry = block(i, carry, True)
    for h in heads:
        acc = carry[h][1]
        out_t = acc[:MLA_V, :] / acc[MLA_V:MLA_V + 1, :]
        o_ref[:, h * MLA_V:(h + 1) * MLA_V] = out_t.T.astype(o_ref.dtype)


def mla_attention(qt, kn, kp, vt, *, batch, seq, tq):
    nq = seq // tq
    t = batch * seq
    nh = MLA_HEADS_PER_STEP
    qd = MLA_NOPE + LANES
    return pl.pallas_call(
        functools.partial(_mla_attn_kernel, tq=tq),
        grid=(batch, MLA_HEADS // nh, nq),
        in_specs=[pl.BlockSpec((None, nh * qd, tq), lambda b, h, i: (b * nq + i, h, 0)),
                  pl.BlockSpec((seq, nh * MLA_NOPE), lambda b, h, i: (b, h)),
                  pl.BlockSpec((seq, LANES), lambda b, h, i: (b, 0)),
                  pl.BlockSpec((nq, nh * MLA_V, tq), lambda b, h, i: (b, h, 0))],
        out_specs=pl.BlockSpec((tq, nh * MLA_V), lambda b, h, i: (b * nq + i, h)),
        out_shape=jax.ShapeDtypeStruct((t, MLA_HEADS * MLA_V), BF16),
        compiler_params=_cparams("parallel", "parallel", "arbitrary"),
    )(qt, kn, kp, vt)


SB_HEADS_PER_STEP = 4


def _sb_attn_kernel(q_ref, k_ref, v_ref, u_ref, o_ref, *, tq):
    i = pl.program_id(2)
    u = u_ref[...]
    hd = SB_HEAD_DIM
    heads = range(SB_HEADS_PER_STEP)

    def block(kb, carries, accs, masked):
        ks = pl.multiple_of(kb * tq, tq)
        if masked:
            row = lax.broadcasted_iota(jnp.int32, (tq, tq), 0)
            col = lax.broadcasted_iota(jnp.int32, (tq, tq), 1)
            keep = col < row
        cols = [slice(h * hd, (h + 1) * hd) for h in heads]
        zs = [_dot_nt(q_ref[:, c], k_ref[pl.ds(ks, tq), c]) for c in cols]
        sps = [_softplus_log2(z) for z in zs]
        if masked:
            sps = [jnp.where(keep, sp, 0.0) for sp in sps]
        his = [sp.astype(BF16) for sp in sps]
        los = [(sp - hi.astype(F32)).astype(BF16) for sp, hi in zip(sps, his)]
        rsums = [_dot(hi, u) + _dot(lo, u) for hi, lo in zip(his, los)]
        ws = []
        for h in heads:
            carry_b = jnp.concatenate([carries[h]] * (tq // LANES), axis=1)
            w = jnp.exp2(zs[h] - rsums[h] - carry_b)
            ws.append(jnp.where(keep, w, 0.0) if masked else w)
        new_accs = [accs[h] + _dot(ws[h].astype(BF16), v_ref[pl.ds(ks, tq), cols[h]]) for h in heads]
        new_carries = [carries[h] + jnp.broadcast_to(rsums[h][:, 0:1], (tq, LANES)) for h in heads]
        return tuple(new_carries), tuple(new_accs)

    def live(carries):
        return jnp.min(functools.reduce(jnp.minimum, carries))

    zeros = tuple(jnp.zeros((tq, LANES), F32) for _ in heads)
    carries, accs = block(i, zeros, zeros, True)

    def cond(st):
        kb, least, _, _ = st
        return jnp.logical_and(kb >= 0, least < SB_DEAD_LOG2)

    def body(st):
        kb, _, carries, accs = st
        carries, accs = block(kb, carries, accs, False)
        return kb - 1, live(carries), carries, accs

    _, _, _, accs = lax.while_loop(cond, body, (i - 1, live(carries), carries, accs))
    for h in heads:
        o_ref[:, h * hd:(h + 1) * hd] = accs[h].astype(o_ref.dtype)


def sb_attention(qkv, *, batch, seq, tq=256):
    tq = min(tq, seq)
    nq = seq // tq
    t = batch * seq
    w = SB_HEADS_PER_STEP * SB_HEAD_DIM
    nhb = SB_HEADS // SB_HEADS_PER_STEP
    u = jnp.asarray(np.tril(np.ones((tq, tq), np.float32)), BF16)
    return pl.pallas_call(
        functools.partial(_sb_attn_kernel, tq=tq),
        grid=(batch, nhb, nq),
        in_specs=[pl.BlockSpec((tq, w), lambda b, h, i: (b * nq + i, h)),
                  pl.BlockSpec((seq, w), lambda b, h, i: (b, nhb + h)),
                  pl.BlockSpec((seq, w), lambda b, h, i: (b, 2 * nhb + h)),
                  pl.BlockSpec((tq, tq), lambda b, h, i: (0, 0))],
        out_specs=pl.BlockSpec((tq, w), lambda b, h, i: (b * nq + i, h)),
        out_shape=jax.ShapeDtypeStruct((t, SB_HEADS * SB_HEAD_DIM), BF16),
        compiler_params=_cparams("parallel", "parallel", "arbitrary"),
    )(qkv, qkv, qkv, u)


def _rope_tables(seq):
    half = MLA_ROPE // 2
    inv_freq = ROPE_THETA ** (-jnp.arange(half, dtype=F32) / half)
    ang = jnp.arange(seq, dtype=F32)[:, None] * inv_freq[None, :]
    cos, sin = jnp.cos(ang), jnp.sin(ang)
    zero = jnp.zeros_like(cos)
    pad = jnp.zeros((seq, LANES - MLA_ROPE), F32)
    c = jnp.concatenate([cos, cos, pad], axis=1)
    sa = jnp.concatenate([-sin, zero, pad], axis=1)
    sb = jnp.concatenate([zero, sin, pad], axis=1)
    return c, sa, sb, cos.T, sin.T


def _pad_lanes(v, width=LANES):
    v = v.reshape(1, -1).astype(F32)
    return jnp.pad(v, ((0, 0), (0, width - v.shape[1])))


def _even_layer(x, mix_g, w_in, conv_w, conv_b, dt_bias, a_log, d_skip, ssd_norm, q_norm, kv_norm,
                w_uq, w_ukv, w_out_stack, e, rope, *, batch, seq):
    d = x.shape[1]
    o_z, o_xbc, o_dt, o_cq, o_ckv, o_kr = np.cumsum(
        [0, SSD_WIDTH, SSD_CONV_CH, SSD_HEADS, MLA_RANK, MLA_RANK]).tolist()
    w_main = jnp.concatenate([w_in[:, o_z:o_xbc], w_in[:, o_cq:o_ckv], w_in[:, o_ckv:o_kr],
                              w_in[:, o_xbc:o_dt]], axis=1).astype(BF16)
    zeros = lambda n: jnp.zeros((d, n), F32)
    w_small = jnp.concatenate([w_in[:, o_kr:o_kr + MLA_ROPE], zeros(LANES - MLA_ROPE),
                               w_in[:, o_dt:o_cq], zeros(LANES - SSD_HEADS)], axis=1).astype(BF16)
    proj, small = norm_matmul(x, mix_g, w_main, w_small)

    y_ssd = ssd_mixer(proj, small, conv_w, conv_b.reshape(1, -1), _pad_lanes(dt_bias), _pad_lanes(a_log),
                      jnp.repeat(d_skip, SSD_HEAD_DIM).reshape(1, -1), ssd_norm.reshape(1, -1),
                      batch=batch, seq=seq, z_blk=0, xbc_blk=1, dt_blk=1)

    wq3 = w_uq.reshape(MLA_RANK, MLA_HEADS, MLA_NOPE + MLA_ROPE)
    wq_t = jnp.pad(wq3, ((0, 0), (0, 0), (0, LANES - MLA_ROPE))).reshape(MLA_RANK, -1).T.astype(BF16)
    wkv3 = w_ukv.reshape(MLA_RANK, MLA_HEADS, MLA_NOPE + MLA_V)
    wkn = wkv3[:, :, :MLA_NOPE].reshape(MLA_RANK, -1).astype(BF16)
    wv_t = wkv3[:, :, MLA_NOPE:].reshape(MLA_RANK, -1).T.astype(BF16)
    tq = min(MLA_TILE, seq)
    qt, kn, vt, kp = mla_prep(proj, small, q_norm.reshape(1, -1), kv_norm.reshape(1, -1), wq_t, wkn, wv_t,
                              *rope, seq=seq, cq_blk=SSD_WIDTH // MLA_RANK,
                              ckv_blk=SSD_WIDTH // MLA_RANK + 1, kr_blk=0, tm=tq)
    y_mla = mla_attention(qt, kn, kp, vt, batch=batch, seq=seq, tq=tq)

    w_ssd = layer_weight_bf16(w_out_stack, e, row0=0, rows=SSD_WIDTH)
    w_mla = layer_weight_bf16(w_out_stack, e, row0=SSD_WIDTH)
    return matmul_residual([y_ssd, y_mla], [w_ssd, w_mla], x)


def _odd_layer(x, mix_g, w_qkv, w_out_stack, o, *, batch, seq):
    width = SB_HEADS * SB_HEAD_DIM
    scale = SB_HEAD_DIM ** -0.5 * LOG2E
    w = jnp.concatenate([w_qkv[:, :width] * scale, w_qkv[:, width:]], axis=1).astype(BF16)
    qkv = norm_matmul(x, mix_g, w)
    y = sb_attention(qkv, batch=batch, seq=seq)
    return matmul_residual([y], [layer_weight_bf16(w_out_stack, o)], x)


def _ffn(x, g, w_gate, w_up, w_down, layer, final_g=None):
    wg, wu, wd = (layer_weight_bf16(w, layer) for w in (w_gate, w_up, w_down))
    return ffn(x, g, wg, wu, wd, final_g)


def kernel(x, mix_norm, ffn_norm, w_in, conv_w, conv_b, dt_bias, a_log, d_skip, ssd_norm, q_norm, kv_norm,
           w_uq, w_ukv, w_out_even, w_qkv, w_out_odd, w_gate, w_up, w_down, final_norm):
    batch, seq, d = x.shape
    depth = mix_norm.shape[0]
    rope = _rope_tables(seq)
    h = x.reshape(batch * seq, d)
    for layer in range(depth):
        mix_g = mix_norm[layer].reshape(1, -1)
        if layer % 2 == 0:
            e = layer // 2
            h = _even_layer(h, mix_g, w_in[e], conv_w[e], conv_b[e], dt_bias[e], a_log[e], d_skip[e],
                            ssd_norm[e], q_norm[e], kv_norm[e], w_uq[e], w_ukv[e], w_out_even, e, rope,
                            batch=batch, seq=seq)
        else:
            o = layer // 2
            h = _odd_layer(h, mix_g, w_qkv[o], w_out_odd, o, batch=batch, seq=seq)
        final_g = final_norm.reshape(1, -1) if layer == depth - 1 else None
        h = _ffn(h, ffn_norm[layer].reshape(1, -1), w_gate, w_up, w_down, layer, final_g)
    return h.reshape(batch, seq, d)
```
